```python
import jax
import jax.numpy as jnp
from jax import lax
import numpy as np

D_MODEL = 1024
BATCH = 8
SEQ = 8192
DEPTH = 1

ROPE_THETA = 500000.0
NORM_EPS = 1e-6
MLA_HEADS = 8
MLA_NOPE = 64
MLA_ROPE = 32
MLA_V = 64
KV_LORA = 4 * MLA_V
Q_LORA = 3 * KV_LORA
ATTN_Q_BLOCK = 128
MOBA_HEADS = 8
MOBA_HD = 64
MOBA_ROT = MOBA_HD // 4
MOBA_BLOCK = 256
MOBA_TOPK = 3
MOBA_Q_CHUNK = 32
N_GROUPS = 4
EXPERTS_PER_GROUP = 8
N_EXPERTS = N_GROUPS * EXPERTS_PER_GROUP
EXPERT_TOPK = 2
EXPERT_FF = 256
EXPERT_BLOCK = 256
IN_SPLITS = (Q_LORA, KV_LORA, MLA_ROPE,
             MOBA_HEADS * MOBA_HD, MOBA_HEADS * MOBA_HD, MOBA_HEADS * MOBA_HD,
             D_MODEL, D_MODEL)
IN_WIDTH = sum(IN_SPLITS)

kernel_name = 'hybrid_mla_moba_hmoe_block'


def rmsnorm(x, g):
    xf = x.astype(jnp.float32)
    y = xf * lax.rsqrt(jnp.mean(xf * xf, axis=-1, keepdims=True) + NORM_EPS)
    return (y * g.astype(jnp.float32)).astype(x.dtype)


def rope_tables(seq, rot_dim):
    half = rot_dim // 2
    inv_freq = jnp.power(ROPE_THETA, -jnp.arange(half, dtype=jnp.float32) / half)
    ang = jnp.arange(seq, dtype=jnp.float32)[:, None] * inv_freq[None, :]
    return jnp.cos(ang), jnp.sin(ang)


def apply_rope(x, cos, sin):
    half = x.shape[-1] // 2
    xf = x.astype(jnp.float32)
    x1, x2 = xf[..., :half], xf[..., half:]
    c = cos[None, :, None, :]
    s = sin[None, :, None, :]
    return jnp.concatenate([x1 * c - x2 * s, x2 * c + x1 * s], axis=-1).astype(x.dtype)


def causal_dense_attention(q, k, v):
    B, S, H, Dq = q.shape
    Dv = v.shape[-1]
    scale = Dq ** -0.5
    nqb = S // ATTN_Q_BLOCK
    q_blocks = q.reshape(B, nqb, ATTN_Q_BLOCK, H, Dq).transpose(1, 0, 2, 3, 4)
    kpos = jnp.arange(S)

    def block(args):
        qi, i = args
        s = jnp.einsum('bqhd,bkhd->bhqk', qi, k).astype(jnp.float32) * scale
        qpos = i * ATTN_Q_BLOCK + jnp.arange(ATTN_Q_BLOCK)
        s = jnp.where(kpos[None, :] <= qpos[:, None], s, -jnp.inf)
        p = jax.nn.softmax(s, axis=-1).astype(v.dtype)
        return jnp.einsum('bhqk,bkhd->bqhd', p, v)

    o = lax.map(block, (q_blocks, jnp.arange(nqb)))
    return o.transpose(1, 0, 2, 3, 4).reshape(B, S, H * Dv)


def mla_attention(c_q, c_kv, k_r, q_norm, w_uq, kv_norm, w_ukv, cos, sin):
    B, S, _ = c_q.shape
    q = (rmsnorm(c_q, q_norm) @ w_uq).reshape(B, S, MLA_HEADS, MLA_NOPE + MLA_ROPE)
    q = jnp.concatenate([q[..., :MLA_NOPE], apply_rope(q[..., MLA_NOPE:], cos, sin)], axis=-1)
    kv = (rmsnorm(c_kv, kv_norm) @ w_ukv).reshape(B, S, MLA_HEADS, MLA_NOPE + MLA_V)
    k_nope, v = kv[..., :MLA_NOPE], kv[..., MLA_NOPE:]
    k_rope = apply_rope(k_r[:, :, None, :], cos, sin)
    k = jnp.concatenate([k_nope, jnp.broadcast_to(k_rope, (B, S, MLA_HEADS, MLA_ROPE))], axis=-1)
    return causal_dense_attention(q, k, v)


def moba_attention(q, k, v, cos, sin):
    B, S, H, dh = q.shape
    scale = dh ** -0.5
    q = jnp.concatenate([apply_rope(q[..., :MOBA_ROT], cos, sin), q[..., MOBA_ROT:]], axis=-1)
    k = jnp.concatenate([apply_rope(k[..., :MOBA_ROT], cos, sin), k[..., MOBA_ROT:]], axis=-1)
    nb = -(-S // MOBA_BLOCK)
    pad = nb * MOBA_BLOCK - S
    kp = jnp.pad(k, ((0, 0), (0, pad), (0, 0), (0, 0)))
    vp = jnp.pad(v, ((0, 0), (0, pad), (0, 0), (0, 0)))
    k_blocks = kp.reshape(B, nb, MOBA_BLOCK, H, dh).transpose(0, 3, 1, 2, 4)
    v_blocks = vp.reshape(B, nb, MOBA_BLOCK, H, dh).transpose(0, 3, 1, 2, 4)
    k_mean = jnp.mean(k_blocks.astype(jnp.float32), axis=3)
    n_sel = min(MOBA_TOPK, nb)
    gather = jax.vmap(jax.vmap(lambda blocks, idx: blocks[idx]))
    nqc = S // MOBA_Q_CHUNK
    q_chunks = q.reshape(B, nqc, MOBA_Q_CHUNK, H, dh).transpose(1, 0, 3, 2, 4)
    blk_ids = jnp.arange(nb)
    sel_ids = jnp.arange(n_sel)

    def chunk(args):
        qc, c = args
        q0 = c * MOBA_Q_CHUNK
        own = q0 // MOBA_BLOCK
        qpos = q0 + jnp.arange(MOBA_Q_CHUNK)
        gate = jnp.einsum('bhqd,bhnd->bhqn', qc.astype(jnp.float32), k_mean)
        gate = jnp.where(blk_ids < own, gate, -jnp.inf)
        _, idx = lax.top_k(gate, n_sel)
        valid = sel_ids < own
        k_sel = gather(k_blocks, idx)
        v_sel = gather(v_blocks, idx)
        s_sel = jnp.einsum('bhqd,bhqjkd->bhqjk', qc, k_sel).astype(jnp.float32) * scale
        s_sel = jnp.where(valid[:, None], s_sel, -jnp.inf).reshape(B, H, MOBA_Q_CHUNK, n_sel * MOBA_BLOCK)
        k_own = lax.dynamic_index_in_dim(k_blocks, own, axis=2, keepdims=False)
        v_own = lax.dynamic_index_in_dim(v_blocks, own, axis=2, keepdims=False)
        kpos = own * MOBA_BLOCK + jnp.arange(MOBA_BLOCK)
        s_own = jnp.einsum('bhqd,bhkd->bhqk', qc, k_own).astype(jnp.float32) * scale
        s_own = jnp.where(kpos[None, :] <= qpos[:, None], s_own, -jnp.inf)
        p = jax.nn.softmax(jnp.concatenate([s_own, s_sel], axis=-1), axis=-1).astype(v.dtype)
        p_own = p[..., :MOBA_BLOCK]
        p_sel = p[..., MOBA_BLOCK:].reshape(B, H, MOBA_Q_CHUNK, n_sel, MOBA_BLOCK)
        return (jnp.einsum('bhqk,bhkd->bhqd', p_own, v_own)
                + jnp.einsum('bhqjk,bhqjkd->bhqd', p_sel, v_sel))

    o = lax.map(chunk, (q_chunks, jnp.arange(nqc)))
    return o.transpose(1, 0, 3, 2, 4).reshape(B, S, H * dh)


def hierarchical_moe(h, w_rg, b_rg, w_re, b_re, w_g, w_u, w_d):
    B, S, D = h.shape
    T = B * S
    hf = h.reshape(T, D)
    g_logits = (hf @ w_rg + b_rg).astype(jnp.float32)
    g_sel = jnp.argmax(g_logits, axis=-1).astype(jnp.int32)
    p_group = jnp.take_along_axis(jax.nn.softmax(g_logits, axis=-1), g_sel[:, None], axis=1)
    e_logits = (jnp.einsum('td,gde->tge', hf, w_re) + b_re).astype(jnp.float32)
    e_in = jnp.take_along_axis(e_logits, g_sel[:, None, None], axis=1)[:, 0]
    top_vals, top_idx = lax.top_k(e_in, EXPERT_TOPK)
    weight = p_group * jax.nn.softmax(top_vals, axis=-1)
    expert_id = g_sel[:, None] * EXPERTS_PER_GROUP + top_idx.astype(jnp.int32)

    A = T * EXPERT_TOPK
    eid = expert_id.reshape(A)
    w_flat = weight.reshape(A)
    tok = jnp.repeat(jnp.arange(T, dtype=jnp.int32), EXPERT_TOPK)
    order = jnp.argsort(eid, stable=True)
    se = eid[order]
    counts = jnp.bincount(eid, length=N_EXPERTS).astype(jnp.int32)
    padded = (counts + EXPERT_BLOCK - 1) // EXPERT_BLOCK * EXPERT_BLOCK
    start = jnp.cumsum(counts) - counts
    pend = jnp.cumsum(padded)
    pstart = pend - padded
    dest = pstart[se] + jnp.arange(A, dtype=jnp.int32) - start[se]
    n_slots = (-(-A // EXPERT_BLOCK) + N_EXPERTS) * EXPERT_BLOCK
    slot_tok = jnp.zeros((n_slots,), jnp.int32).at[dest].set(tok[order])
    slot_w = jnp.zeros((n_slots,), h.dtype).at[dest].set(w_flat[order].astype(h.dtype))
    n_blocks = n_slots // EXPERT_BLOCK
    blk_e = jnp.minimum(jnp.searchsorted(pend, jnp.arange(n_blocks, dtype=jnp.int32) * EXPERT_BLOCK,
                                         side='right'), N_EXPERTS - 1)
    xs = hf[slot_tok].reshape(n_blocks, EXPERT_BLOCK, D)

    def expert_block(args):
        xb, e = args
        return (jax.nn.silu(xb @ w_g[e]) * (xb @ w_u[e])) @ w_d[e]

    ys = lax.map(expert_block, (xs, blk_e)).reshape(n_slots, D)
    out = jnp.zeros((T, D), h.dtype).at[slot_tok].add(ys * slot_w[:, None])
    return out.reshape(B, S, D)


def setup_inputs(seed: int = 0) -> dict:
    key = jax.random.key(seed)
    ks = jax.random.split(key, 20)
    f32 = jnp.float32
    L = DEPTH

    def dense(k, shape, fan_in):
        return jax.random.normal(k, shape, f32) * fan_in ** -0.5

    def gain(k, shape):
        return 1.0 + 0.02 * jax.random.normal(k, shape, f32)

    return {
        'x': jax.random.normal(ks[0], (BATCH, SEQ, D_MODEL), f32),
        'attn_norm': gain(ks[1], (L, D_MODEL)),
        'w_in': dense(ks[2], (L, D_MODEL, IN_WIDTH), D_MODEL),
        'q_norm': gain(ks[3], (L, Q_LORA)),
        'w_uq': dense(ks[4], (L, Q_LORA, MLA_HEADS * (MLA_NOPE + MLA_ROPE)), Q_LORA),
        'kv_norm': gain(ks[5], (L, KV_LORA)),
        'w_ukv': dense(ks[6], (L, KV_LORA, MLA_HEADS * (MLA_NOPE + MLA_V)), KV_LORA),
        'w_o_mla': dense(ks[7], (L, MLA_HEADS * MLA_V, D_MODEL), MLA_HEADS * MLA_V),
        'w_o_moba': dense(ks[8], (L, MOBA_HEADS * MOBA_HD, D_MODEL), MOBA_HEADS * MOBA_HD),
        'w_out': dense(ks[9], (L, D_MODEL, D_MODEL), D_MODEL),
        'ffn_norm': gain(ks[10], (L, D_MODEL)),
        'w_router_group': dense(ks[11], (L, D_MODEL, N_GROUPS), D_MODEL),
        'b_router_group': 0.01 * jax.random.normal(ks[12], (L, N_GROUPS), f32),
        'w_router_expert': dense(ks[13], (L, N_GROUPS, D_MODEL, EXPERTS_PER_GROUP), D_MODEL),
        'b_router_expert': 0.01 * jax.random.normal(ks[14], (L, N_GROUPS, EXPERTS_PER_GROUP), f32),
        'w_exp_gate': dense(ks[15], (L, N_EXPERTS, D_MODEL, EXPERT_FF), D_MODEL),
        'w_exp_up': dense(ks[16], (L, N_EXPERTS, D_MODEL, EXPERT_FF), D_MODEL),
        'w_exp_down': dense(ks[17], (L, N_EXPERTS, EXPERT_FF, D_MODEL), EXPERT_FF),
        'final_norm': gain(ks[18], (D_MODEL,)),
    }


def reference(x, attn_norm, w_in, q_norm, w_uq, kv_norm, w_ukv, w_o_mla, w_o_moba, w_out,
              ffn_norm, w_router_group, b_router_group, w_router_expert, b_router_expert,
              w_exp_gate, w_exp_up, w_exp_down, final_norm):
    B, S, _ = x.shape
    cos_mla, sin_mla = rope_tables(S, MLA_ROPE)
    cos_moba, sin_moba = rope_tables(S, MOBA_ROT)
    split_at = [int(i) for i in np.cumsum(IN_SPLITS)[:-1]]
    for l in range(DEPTH):
        h = rmsnorm(x, attn_norm[l])
        proj = h @ w_in[l]
        c_q, c_kv, k_r, q_b, k_b, v_b, gate_a, gate_b = jnp.split(proj, split_at, axis=-1)
        o_a = mla_attention(c_q, c_kv, k_r, q_norm[l], w_uq[l], kv_norm[l], w_ukv[l], cos_mla, sin_mla)
        o_b = moba_attention(q_b.reshape(B, S, MOBA_HEADS, MOBA_HD),
                             k_b.reshape(B, S, MOBA_HEADS, MOBA_HD),
                             v_b.reshape(B, S, MOBA_HEADS, MOBA_HD), cos_moba, sin_moba)
        mixed = (jax.nn.sigmoid(gate_a) * (o_a @ w_o_mla[l])
                 + jax.nn.sigmoid(gate_b) * (o_b @ w_o_moba[l]))
        x = x + mixed @ w_out[l]
        x = x + hierarchical_moe(rmsnorm(x, ffn_norm[l]), w_router_group[l], b_router_group[l],
                                 w_router_expert[l], b_router_expert[l],
                                 w_exp_gate[l], w_exp_up[l], w_exp_down[l])
    return rmsnorm(x, final_norm)
```

```python
import functools

import jax
import jax.numpy as jnp
import numpy as np
from jax import lax
from jax.experimental import pallas as pl
from jax.experimental.pallas import tpu as pltpu

F32 = jnp.float32
BF16 = jnp.bfloat16

D_MODEL = 1024
ROPE_THETA = 500000.0
NORM_EPS = 1e-6
MLA_HEADS = 8
MLA_NOPE = 64
MLA_ROPE = 32
MLA_V = 64
KV_LORA = 256
Q_LORA = 768
MOBA_HEADS = 8
MOBA_HD = 64
MOBA_ROT = 16
MOBA_BLOCK = 256
MOBA_TOPK = 3
N_GROUPS = 4
EXPERTS_PER_GROUP = 8
N_EXPERTS = 32
EXPERT_TOPK = 2
EXPERT_FF = 256
EXPERT_BLOCK = 256

LANES = 128
ROW_SLABS = D_MODEL // LANES
HEAD_PAIRS = MLA_HEADS // 2
NEG_BIG = -1e30
VMEM_LIMIT = 56 * 1024 * 1024

SEG_CQ = (0, 768)
SEG_CKV = (768, 1024)
SEG_KR = (1024, 1152)
SEG_QB = (1152, 1664)
SEG_KB = (1664, 2176)
SEG_VB = (2176, 2688)
SEG_GA = (2688, 3712)
SEG_GB = (3712, 4736)
IN_WIDTH_R = 4736


def _dot(a, b):
    return jnp.dot(a, b, preferred_element_type=F32)


def _dot_nt(a, b):
    return lax.dot_general(a, b, (((1,), (1,)), ((), ())), preferred_element_type=F32)


def _rms(x, g):
    return x * lax.rsqrt(jnp.mean(x * x, axis=-1, keepdims=True) + NORM_EPS) * g


def _rope_tile(x, c, s_lo, s_hi, half):
    return x * c + pltpu.roll(x, LANES - half, 1) * s_lo + pltpu.roll(x, half, 1) * s_hi


def _proj_body(x_ref, g_ref, win_ref, qn_ref, wuq_ref, kvn_ref, wuk_ref, wuv_ref,
               mc_ref, ms1_ref, ms2_ref, bc_ref, bs1_ref, bs2_ref,
               qm_ref, km_ref, vm_ref, qb_ref, kb_ref, vb_ref, kmean_ref, ga_ref, gb_ref):
    tm = x_ref.shape[0]
    h = _rms(x_ref[...], g_ref[...]).astype(BF16)

    def seg(s):
        return _dot(h, win_ref[:, s[0]:s[1]])

    mc, ms1, ms2 = mc_ref[...], ms1_ref[...], ms2_ref[...]
    bc, bs1, bs2 = bc_ref[...], bs1_ref[...], bs2_ref[...]

    hq = _rms(seg(SEG_CQ), qn_ref[...]).astype(BF16)
    q = _dot(hq, wuq_ref[...])
    q_scale = (MLA_NOPE + MLA_ROPE) ** -0.5
    for hd in range(MLA_HEADS):
        sl = slice(hd * LANES, (hd + 1) * LANES)
        qm_ref[:, sl] = (_rope_tile(q[:, sl], mc, ms1, ms2, MLA_ROPE // 2) * q_scale).astype(BF16)

    hkv = _rms(seg(SEG_CKV), kvn_ref[...]).astype(BF16)
    k_nope = _dot(hkv, wuk_ref[...])
    kr = _rope_tile(seg(SEG_KR), mc, ms1, ms2, MLA_ROPE // 2)
    for hd in range(MLA_HEADS):
        sl = slice(hd * LANES, (hd + 1) * LANES)
        km_ref[:, sl] = (k_nope[:, sl] + kr).astype(BF16)
    vm_ref[...] = _dot(hkv, wuv_ref[...]).astype(BF16)

    qb = seg(SEG_QB)
    kb = seg(SEG_KB)
    qb_scale = MOBA_HD ** -0.5
    n_blk = tm // MOBA_BLOCK
    for t in range(MOBA_HEADS * MOBA_HD // LANES):
        sl = slice(t * LANES, (t + 1) * LANES)
        qb_ref[:, sl] = (_rope_tile(qb[:, sl], bc, bs1, bs2, MOBA_ROT // 2) * qb_scale).astype(BF16)
        kt = _rope_tile(kb[:, sl], bc, bs1, bs2, MOBA_ROT // 2)
        kb_ref[:, sl] = kt.astype(BF16)
        for j in range(n_blk):
            kmean_ref[j, :, sl] = jnp.mean(kt[j * MOBA_BLOCK:(j + 1) * MOBA_BLOCK], axis=0, keepdims=True)
    vb_ref[...] = seg(SEG_VB).astype(BF16)
    ga_ref[...] = seg(SEG_GA).astype(BF16)
    gb_ref[...] = seg(SEG_GB).astype(BF16)


def _proj_call(x2d, g_attn, w_in_r, q_norm, w_uq_r, kv_norm, w_uk_r, w_uv_r, tabs, seq, tm, interpret):
    T = x2d.shape[0]
    n_seq_tiles = seq // tm
    row = lambda i: (i, 0)
    const = lambda i: (0, 0)
    tab = lambda i: (i % n_seq_tiles, 0)
    full = lambda a: pl.BlockSpec(a.shape, const)
    n_blk = tm // MOBA_BLOCK
    out_shapes = [
        jax.ShapeDtypeStruct((T, MLA_HEADS * LANES), BF16),
        jax.ShapeDtypeStruct((T, MLA_HEADS * LANES), BF16),
        jax.ShapeDtypeStruct((T, MLA_HEADS * MLA_V), BF16),
        jax.ShapeDtypeStruct((T, 512), BF16),
        jax.ShapeDtypeStruct((T, 512), BF16),
        jax.ShapeDtypeStruct((T, 512), BF16),
        jax.ShapeDtypeStruct((T // MOBA_BLOCK, 1, 512), F32),
        jax.ShapeDtypeStruct((T, D_MODEL), BF16),
        jax.ShapeDtypeStruct((T, D_MODEL), BF16),
    ]
    out_specs = [
        pl.BlockSpec((tm, 1024), row), pl.BlockSpec((tm, 1024), row), pl.BlockSpec((tm, 512), row),
        pl.BlockSpec((tm, 512), row), pl.BlockSpec((tm, 512), row), pl.BlockSpec((tm, 512), row),
        pl.BlockSpec((n_blk, 1, 512), lambda i: (i, 0, 0)),
        pl.BlockSpec((tm, 1024), row), pl.BlockSpec((tm, 1024), row),
    ]
    in_specs = [pl.BlockSpec((tm, D_MODEL), row), full(g_attn), full(w_in_r), full(q_norm), full(w_uq_r),
                full(kv_norm), full(w_uk_r), full(w_uv_r)] + [pl.BlockSpec((tm, LANES), tab)] * 6
    return pl.pallas_call(
        _proj_body, grid=(T // tm,), in_specs=in_specs, out_specs=out_specs, out_shape=out_shapes,
        compiler_params=pltpu.CompilerParams(dimension_semantics=("parallel",), vmem_limit_bytes=VMEM_LIMIT),
        name="proj", interpret=interpret,
    )(x2d, g_attn, w_in_r, q_norm, w_uq_r, kv_norm, w_uk_r, w_uv_r, *tabs)


def _softmax_step(s, m, l):
    m_new = jnp.maximum(m, jnp.max(s, axis=-1, keepdims=True))
    alpha = jnp.exp(m - m_new)
    p = jnp.exp(s - m_new)
    return m_new, alpha, alpha * l + jnp.sum(p, axis=-1, keepdims=True), p.astype(BF16)


def _pair_update(state, s0, s1, v, lo):
    m0, l0, m1, l1, acc = state
    m0, a0, l0, p0 = _softmax_step(s0, m0, l0)
    m1, a1, l1, p1 = _softmax_step(s1, m1, l1)
    zero = jnp.zeros_like(v)
    acc = acc * jnp.where(lo, a0, a1) + _dot(p0, jnp.where(lo, v, zero)) + _dot(p1, jnp.where(lo, zero, v))
    return m0, l0, m1, l1, acc


def _init_state(tq):
    col = lambda val: jnp.full((tq, 1), val, F32)
    return col(-jnp.inf), col(0.0), col(-jnp.inf), col(0.0), jnp.zeros((tq, LANES), F32)


def _mla_body(q0_ref, q1_ref, k0_ref, k1_ref, v_ref, o_ref):
    tq = q0_ref.shape[0]
    qi = pl.program_id(2)
    lo = lax.broadcasted_iota(jnp.int32, (1, LANES), 1) < MLA_V
    q0, q1 = q0_ref[...], q1_ref[...]

    def scores(start):
        rows = pl.ds(start, tq)
        return _dot_nt(q0, k0_ref[rows, :]), _dot_nt(q1, k1_ref[rows, :]), v_ref[rows, :]

    def past(j, state):
        s0, s1, v = scores(pl.multiple_of(j * tq, tq))
        return _pair_update(state, s0, s1, v, lo)

    state = lax.fori_loop(0, qi, past, _init_state(tq))
    s0, s1, v = scores(pl.multiple_of(qi * tq, tq))
    causal = (lax.broadcasted_iota(jnp.int32, (tq, tq), 1) <= lax.broadcasted_iota(jnp.int32, (tq, tq), 0))
    s0 = jnp.where(causal, s0, -jnp.inf)
    s1 = jnp.where(causal, s1, -jnp.inf)
    _, l0, _, l1, acc = _pair_update(state, s0, s1, v, lo)
    o_ref[...] = (acc / jnp.where(lo, l0, l1)).astype(o_ref.dtype)


def _mla_call(q_mla, k_mla, v_mla, batch, seq, tq, interpret):
    T = q_mla.shape[0]
    nq = seq // tq
    qspec = lambda off: pl.BlockSpec((tq, LANES), lambda b, p, i: (b * nq + i, 2 * p + off))
    kspec = lambda off: pl.BlockSpec((seq, LANES), lambda b, p, i: (b, 2 * p + off))
    return pl.pallas_call(
        _mla_body, grid=(batch, HEAD_PAIRS, nq),
        in_specs=[qspec(0), qspec(1), kspec(0), kspec(1), pl.BlockSpec((seq, LANES), lambda b, p, i: (b, p))],
        out_specs=pl.BlockSpec((tq, LANES), lambda b, p, i: (b * nq + i, p)),
        out_shape=jax.ShapeDtypeStruct((T, MLA_HEADS * MLA_V), BF16),
        compiler_params=pltpu.CompilerParams(dimension_semantics=("parallel", "parallel", "arbitrary"),
                                             vmem_limit_bytes=VMEM_LIMIT),
        name="mla_attn", interpret=interpret,
    )(q_mla, q_mla, k_mla, k_mla, v_mla)


def _top_blocks_bias(gate, n_past):
    lane = lax.broadcasted_iota(jnp.int32, gate.shape, 1)
    g = jnp.where(lane < n_past, gate, -jnp.inf)
    sel = jnp.zeros(gate.shape, jnp.bool_)
    for _ in range(MOBA_TOPK):
        mx = jnp.max(g, axis=-1, keepdims=True)
        cand = (g == mx) & (mx > -jnp.inf)
        first = jnp.min(jnp.where(cand, lane, LANES), axis=-1, keepdims=True)
        pick = lane == first
        sel = sel | pick
        g = jnp.where(pick, -jnp.inf, g)
    return jnp.where(sel, 0.0, NEG_BIG)


def _moba_body(q_ref, k_ref, v_ref, kmean_ref, o_ref):
    tq = q_ref.shape[0]
    qi = pl.program_id(2)
    lane = lax.broadcasted_iota(jnp.int32, (1, LANES), 1)
    lo = lane < MOBA_HD
    q = q_ref[...]
    zq = jnp.zeros_like(q)
    q0, q1 = jnp.where(lo, q, zq), jnp.where(lo, zq, q)

    kmean = kmean_ref[0]
    kmean = jnp.concatenate([kmean, jnp.zeros((LANES - kmean.shape[0], LANES), F32)], axis=0)

    def aug(qh):
        gate = lax.dot_general(qh.astype(F32), kmean, (((1,), (1,)), ((), ())),
                               precision=lax.Precision.HIGHEST, preferred_element_type=F32)
        return jnp.concatenate([qh, _top_blocks_bias(gate, qi).astype(BF16)], axis=1)

    qa0, qa1 = aug(q0), aug(q1)

    own = pl.ds(pl.multiple_of(qi * tq, tq), tq)
    k_own = k_ref[own, :]
    causal = (lax.broadcasted_iota(jnp.int32, (tq, tq), 1) <= lax.broadcasted_iota(jnp.int32, (tq, tq), 0))
    s0 = jnp.where(causal, _dot_nt(q0, k_own), -jnp.inf)
    s1 = jnp.where(causal, _dot_nt(q1, k_own), -jnp.inf)
    state = _pair_update(_init_state(tq), s0, s1, v_ref[own, :], lo)

    def past(j, state):
        rows = pl.ds(pl.multiple_of(j * tq, tq), tq)
        onehot = jnp.broadcast_to((lane == j).astype(BF16), (tq, LANES))
        k_aug = jnp.concatenate([k_ref[rows, :], onehot], axis=1)
        return _pair_update(state, _dot_nt(qa0, k_aug), _dot_nt(qa1, k_aug), v_ref[rows, :], lo)

    _, l0, _, l1, acc = lax.fori_loop(0, qi, past, state)
    o_ref[...] = (acc / jnp.where(lo, l0, l1)).astype(o_ref.dtype)


def _moba_call(q_b, k_b, v_b, kmean, batch, seq, interpret):
    T = q_b.shape[0]
    tq = MOBA_BLOCK
    nq = seq // tq
    kv = pl.BlockSpec((seq, LANES), lambda b, p, i: (b, p))
    return pl.pallas_call(
        _moba_body, grid=(batch, HEAD_PAIRS, nq),
        in_specs=[pl.BlockSpec((tq, LANES), lambda b, p, i: (b * nq + i, p)), kv, kv,
                  pl.BlockSpec((1, nq, LANES), lambda b, p, i: (b, 0, p))],
        out_specs=pl.BlockSpec((tq, LANES), lambda b, p, i: (b * nq + i, p)),
        out_shape=jax.ShapeDtypeStruct((T, MOBA_HEADS * MOBA_HD), BF16),
        compiler_params=pltpu.CompilerParams(dimension_semantics=("parallel", "parallel", "arbitrary"),
                                             vmem_limit_bytes=VMEM_LIMIT),
        name="moba_attn", interpret=interpret,
    )(q_b, k_b, v_b, kmean)


ROUTE_GROUP0 = 0
ROUTE_EXPERT0 = N_GROUPS


def _first_lane_of_max(vals, lane):
    mx = jnp.max(vals, axis=-1, keepdims=True)
    return mx, jnp.min(jnp.where(vals == mx, lane, LANES), axis=-1, keepdims=True)


def _mix_body(oa_ref, ob_ref, ga_ref, gb_ref, x_ref, woa_ref, wob_ref, wout_ref, fn_ref, wr_ref, br_ref,
              x1_ref, h2_ref, route_ref, counts_ref, carry_ref):
    tm = x_ref.shape[0]

    @pl.when(pl.program_id(0) == 0)
    def _():
        carry_ref[...] = jnp.zeros_like(carry_ref)

    a = _dot(oa_ref[...], woa_ref[...])
    b = _dot(ob_ref[...], wob_ref[...])
    mixed = jax.nn.sigmoid(ga_ref[...].astype(F32)) * a + jax.nn.sigmoid(gb_ref[...].astype(F32)) * b
    x1 = x_ref[...] + _dot(mixed.astype(BF16), wout_ref[...])
    x1_ref[...] = x1
    h2 = _rms(x1, fn_ref[...])
    for s in range(ROW_SLABS):
        h2_ref[:, s, :] = h2[:, s * LANES:(s + 1) * LANES]

    logits = jnp.dot(h2, wr_ref[...], precision=lax.Precision.HIGHEST, preferred_element_type=F32) + br_ref[...]
    lane = lax.broadcasted_iota(jnp.int32, (tm, LANES), 1)
    neg = -jnp.inf
    g_log = jnp.where(lane < N_GROUPS, logits, neg)
    g_max, g_sel = _first_lane_of_max(g_log, lane)
    p_group = 1.0 / jnp.sum(jnp.exp(g_log - g_max), axis=-1, keepdims=True)
    e_lo = ROUTE_EXPERT0 + g_sel * EXPERTS_PER_GROUP
    e_log = jnp.where((lane >= e_lo) & (lane < e_lo + EXPERTS_PER_GROUP), logits, neg)
    v1, i1 = _first_lane_of_max(e_log, lane)
    v2, i2 = _first_lane_of_max(jnp.where(lane == i1, neg, e_log), lane)
    e2 = jnp.exp(v2 - v1)
    w1 = p_group * (1.0 / (1.0 + e2))
    w2 = p_group * (e2 / (1.0 + e2))

    pick1, pick2 = lane == i1, lane == i2
    onehot = (pick1 | pick2).astype(F32)
    tri = (lax.broadcasted_iota(jnp.int32, (tm, tm), 1) < lax.broadcasted_iota(jnp.int32, (tm, tm), 0))
    before = _dot(tri.astype(BF16), onehot.astype(BF16)) + carry_ref[0:1, :]
    r1 = jnp.sum(jnp.where(pick1, before, 0.0), axis=-1, keepdims=True)
    r2 = jnp.sum(jnp.where(pick2, before, 0.0), axis=-1, keepdims=True)
    new_carry = carry_ref[0:1, :] + jnp.sum(onehot, axis=0, keepdims=True)
    carry_ref[...] = jnp.broadcast_to(new_carry, carry_ref.shape)
    counts_ref[...] = jnp.broadcast_to(new_carry, counts_ref.shape)

    fi1 = (i1 - ROUTE_EXPERT0).astype(F32)
    fi2 = (i2 - ROUTE_EXPERT0).astype(F32)
    route = jnp.zeros((tm, LANES), F32)
    for k, val in enumerate((fi1, fi2, w1, w2, r1, r2)):
        route = jnp.where(lane == k, val, route)
    route_ref[...] = route


def _mix_call(o_a, o_b, g_a, g_b, x2d, w_oa, w_ob, w_out, ffn_norm, w_router, b_router, tm, interpret):
    T = x2d.shape[0]
    row = lambda i: (i, 0)
    const = lambda i: (0, 0)
    full = lambda a: pl.BlockSpec(a.shape, const)
    return pl.pallas_call(
        _mix_body, grid=(T // tm,),
        in_specs=[pl.BlockSpec((tm, 512), row), pl.BlockSpec((tm, 512), row),
                  pl.BlockSpec((tm, D_MODEL), row), pl.BlockSpec((tm, D_MODEL), row),
                  pl.BlockSpec((tm, D_MODEL), row), full(w_oa), full(w_ob), full(w_out), full(ffn_norm),
                  full(w_router), full(b_router)],
        out_specs=[pl.BlockSpec((tm, D_MODEL), row), pl.BlockSpec((tm, ROW_SLABS, LANES), lambda i: (i, 0, 0)),
                   pl.BlockSpec((tm, LANES), row), pl.BlockSpec((8, LANES), const)],
        out_shape=[jax.ShapeDtypeStruct((T, D_MODEL), F32), jax.ShapeDtypeStruct((T, ROW_SLABS, LANES), F32),
                   jax.ShapeDtypeStruct((T, LANES), F32), jax.ShapeDtypeStruct((8, LANES), F32)],
        scratch_shapes=[pltpu.VMEM((8, LANES), F32)],
        compiler_params=pltpu.CompilerParams(dimension_semantics=("arbitrary",), vmem_limit_bytes=VMEM_LIMIT),
        name="mix_route", interpret=interpret,
    )(o_a, o_b, g_a, g_b, x2d, w_oa, w_ob, w_out, ffn_norm, w_router, b_router)


def _slab_rows(ref):
    return jnp.concatenate([ref[:, s, :] for s in range(ROW_SLABS)], axis=1)


def _gather_rows(src_hbm, idx_ref, n, dst_ref, sem):
    def issue(r, c):
        pltpu.make_async_copy(src_hbm.at[idx_ref[0, 0, r]], dst_ref.at[r], sem).start()
        return c

    lax.fori_loop(0, n, issue, 0)
    pltpu.make_async_copy(src_hbm.at[pl.ds(0, n)], dst_ref, sem).wait()


def _moe_body(blk_e_ref, n_used_ref, tok_ref, w_ref, h_hbm, wg_ref, wu_ref, wd_ref, y_ref, xs_ref, sem):
    del blk_e_ref
    nb = xs_ref.shape[0]
    used = pl.program_id(0) < n_used_ref[0]

    @pl.when(used)
    def _():
        _gather_rows(h_hbm, tok_ref, nb, xs_ref, sem.at[0])
        xs = _slab_rows(xs_ref).astype(BF16)
        act = jax.nn.silu(_dot(xs, wg_ref[0])) * _dot(xs, wu_ref[0])
        ys = _dot(act.astype(BF16), wd_ref[0]) * w_ref[...]
        for s in range(ROW_SLABS):
            y_ref[:, s, :] = ys[:, s * LANES:(s + 1) * LANES]

    @pl.when(jnp.logical_not(used))
    def _():
        y_ref[...] = jnp.zeros_like(y_ref)


def _moe_call(blk_e, n_used, slot_tok, slot_w, h2, w_g, w_u, w_d, interpret):
    n_blocks = slot_tok.shape[0]
    nb = EXPERT_BLOCK
    grid_spec = pltpu.PrefetchScalarGridSpec(
        num_scalar_prefetch=2, grid=(n_blocks,),
        in_specs=[pl.BlockSpec((1, 1, nb), lambda i, be, nu: (i, 0, 0), memory_space=pltpu.SMEM),
                  pl.BlockSpec((nb, 1), lambda i, be, nu: (i, 0)),
                  pl.BlockSpec(memory_space=pl.ANY),
                  pl.BlockSpec((1, D_MODEL, EXPERT_FF), lambda i, be, nu: (be[i], 0, 0)),
                  pl.BlockSpec((1, D_MODEL, EXPERT_FF), lambda i, be, nu: (be[i], 0, 0)),
                  pl.BlockSpec((1, EXPERT_FF, D_MODEL), lambda i, be, nu: (be[i], 0, 0))],
        out_specs=pl.BlockSpec((nb, ROW_SLABS, LANES), lambda i, be, nu: (i, 0, 0)),
        scratch_shapes=[pltpu.VMEM((nb, ROW_SLABS, LANES), F32), pltpu.SemaphoreType.DMA((1,))],
    )
    return pl.pallas_call(
        _moe_body, grid_spec=grid_spec,
        out_shape=jax.ShapeDtypeStruct((n_blocks * nb, ROW_SLABS, LANES), F32),
        compiler_params=pltpu.CompilerParams(dimension_semantics=("arbitrary",), vmem_limit_bytes=VMEM_LIMIT),
        name="moe_experts", interpret=interpret,
    )(blk_e, n_used, slot_tok, slot_w, h2, w_g, w_u, w_d)


def _final_body(dest_ref, x1_ref, y_hbm, g_ref, o_ref, ybuf_ref, sem):
    tm = x1_ref.shape[0]
    _gather_rows(y_hbm, dest_ref, EXPERT_TOPK * tm, ybuf_ref, sem.at[0])
    y = x1_ref[...] + _slab_rows(ybuf_ref.at[pl.ds(0, tm)]) + _slab_rows(ybuf_ref.at[pl.ds(tm, tm)])
    o_ref[...] = _rms(y, g_ref[...])


def _final_call(dest, x1, y, final_norm, tm, interpret):
    T = x1.shape[0]
    return pl.pallas_call(
        _final_body, grid=(T // tm,),
        in_specs=[pl.BlockSpec((1, 1, EXPERT_TOPK * tm), lambda i: (i, 0, 0), memory_space=pltpu.SMEM),
                  pl.BlockSpec((tm, D_MODEL), lambda i: (i, 0)),
                  pl.BlockSpec(memory_space=pl.ANY),
                  pl.BlockSpec((1, D_MODEL), lambda i: (0, 0))],
        out_specs=pl.BlockSpec((tm, D_MODEL), lambda i: (i, 0)),
        out_shape=jax.ShapeDtypeStruct((T, D_MODEL), F32),
        scratch_shapes=[pltpu.VMEM((EXPERT_TOPK * tm, ROW_SLABS, LANES), F32), pltpu.SemaphoreType.DMA((1,))],
        compiler_params=pltpu.CompilerParams(dimension_semantics=("arbitrary",), vmem_limit_bytes=VMEM_LIMIT),
        name="final_norm", interpret=interpret,
    )(dest, x1, y, final_norm)


def _rope_tables(seq, rot_dim):
    half = rot_dim // 2
    inv_freq = jnp.power(ROPE_THETA, -jnp.arange(half, dtype=F32) / half)
    ang = jnp.arange(seq, dtype=F32)[:, None] * inv_freq[None, :]
    return jnp.cos(ang), jnp.sin(ang)


def _lane_tables(seq):
    cos, sin = _rope_tables(seq, MLA_ROPE)
    z = lambda n: jnp.zeros((seq, n), F32)
    o = lambda n: jnp.ones((seq, n), F32)
    mla = (jnp.concatenate([o(64), cos, cos, o(32)], 1),
           jnp.concatenate([z(64), -sin, z(48)], 1),
           jnp.concatenate([z(80), sin, z(32)], 1))
    cos, sin = _rope_tables(seq, MOBA_ROT)
    head = (jnp.concatenate([cos, cos, o(48)], 1), jnp.concatenate([-sin, z(56)], 1),
            jnp.concatenate([z(8), sin, z(48)], 1))
    moba = tuple(jnp.concatenate([t, t], 1) for t in head)
    return mla + moba


def _prep_weights(w_in, w_uq, w_ukv):
    kr = jnp.pad(w_in[:, 1024:1056], ((0, 0), (64, 32)))
    w_in_r = jnp.concatenate([w_in[:, :1024], kr, w_in[:, 1056:]], axis=1).astype(BF16)
    w_uq_r = jnp.pad(w_uq.reshape(Q_LORA, MLA_HEADS, MLA_NOPE + MLA_ROPE),
                     ((0, 0), (0, 0), (0, LANES - MLA_NOPE - MLA_ROPE))).reshape(Q_LORA, MLA_HEADS * LANES)
    w_kv = w_ukv.reshape(KV_LORA, MLA_HEADS, MLA_NOPE + MLA_V)
    w_uk_r = jnp.pad(w_kv[:, :, :MLA_NOPE], ((0, 0), (0, 0), (0, LANES - MLA_NOPE))).reshape(KV_LORA, -1)
    w_uv_r = w_kv[:, :, MLA_NOPE:].reshape(KV_LORA, MLA_HEADS * MLA_V)
    return w_in_r, w_uq_r.astype(BF16), w_uk_r.astype(BF16), w_uv_r.astype(BF16)


def _router_weights(w_rg, b_rg, w_re, b_re):
    w_e = jnp.transpose(w_re, (1, 0, 2)).reshape(D_MODEL, N_EXPERTS)
    w = jnp.concatenate([w_rg, w_e], axis=1)
    b = jnp.concatenate([b_rg, b_re.reshape(N_EXPERTS)])
    pad = LANES - N_GROUPS - N_EXPERTS
    return jnp.pad(w, ((0, 0), (0, pad))), jnp.pad(b, (0, pad))[None, :]


def _dispatch(route, counts, T, tm):
    eid = route[:, 0:2].astype(jnp.int32)
    w = route[:, 2:4]
    rank = route[:, 4:6].astype(jnp.int32)
    cnt = counts[0, ROUTE_EXPERT0:ROUTE_EXPERT0 + N_EXPERTS].astype(jnp.int32)
    padded = (cnt + EXPERT_BLOCK - 1) // EXPERT_BLOCK * EXPERT_BLOCK
    pend = jnp.cumsum(padded)
    pstart = pend - padded
    dest = pstart[eid] + rank
    n_blocks = T * EXPERT_TOPK // EXPERT_BLOCK + N_EXPERTS
    n_slots = n_blocks * EXPERT_BLOCK
    tok = jnp.repeat(jnp.arange(T, dtype=jnp.int32), EXPERT_TOPK)
    slot_tok = jnp.zeros((n_slots,), jnp.int32).at[dest.reshape(-1)].set(tok)
    slot_w = jnp.zeros((n_slots,), F32).at[dest.reshape(-1)].set(w.reshape(-1))
    blk_e = jnp.minimum(jnp.searchsorted(pend, jnp.arange(n_blocks, dtype=jnp.int32) * EXPERT_BLOCK, side='right'),
                        N_EXPERTS - 1).astype(jnp.int32)
    n_used = (pend[-1] // EXPERT_BLOCK).astype(jnp.int32)[None]
    dest_tiles = dest.reshape(T // tm, tm, EXPERT_TOPK).transpose(0, 2, 1).reshape(T // tm, 1, EXPERT_TOPK * tm)
    return blk_e, n_used, slot_tok.reshape(n_blocks, 1, EXPERT_BLOCK), slot_w[:, None], dest_tiles


def _forward(x, attn_norm, w_in, q_norm, w_uq, kv_norm, w_ukv, w_o_mla, w_o_moba, w_out, ffn_norm,
             w_router_group, b_router_group, w_router_expert, b_router_expert, w_exp_gate, w_exp_up,
             w_exp_down, final_norm, *, tm=256, tq_mla=512, interpret=False):
    B, S, _ = x.shape
    T = B * S
    x2d = x.reshape(T, D_MODEL)
    tabs = _lane_tables(S)
    for l in range(attn_norm.shape[0]):
        w_in_r, w_uq_r, w_uk_r, w_uv_r = _prep_weights(w_in[l], w_uq[l], w_ukv[l])
        q_m, k_m, v_m, q_b, k_b, v_b, kmean, g_a, g_b = _proj_call(
            x2d, attn_norm[l][None], w_in_r, q_norm[l][None], w_uq_r, kv_norm[l][None], w_uk_r, w_uv_r,
            tabs, S, tm, interpret)
        o_a = _mla_call(q_m, k_m, v_m, B, S, tq_mla, interpret)
        o_b = _moba_call(q_b, k_b, v_b, kmean.reshape(B, S // MOBA_BLOCK, 512), B, S, interpret)
        w_r, b_r = _router_weights(w_router_group[l], b_router_group[l], w_router_expert[l], b_router_expert[l])
        x1, h2, route, counts = _mix_call(o_a, o_b, g_a, g_b, x2d, w_o_mla[l].astype(BF16),
                                          w_o_moba[l].astype(BF16), w_out[l].astype(BF16), ffn_norm[l][None],
                                          w_r, b_r, tm, interpret)
        blk_e, n_used, slot_tok, slot_w, dest_tiles = _dispatch(route, counts, T, tm)
        y = _moe_call(blk_e, n_used, slot_tok, slot_w, h2, w_exp_gate[l].astype(BF16),
                      w_exp_up[l].astype(BF16), w_exp_down[l].astype(BF16), interpret)
        assert attn_norm.shape[0] == 1
    return _final_call(dest_tiles, x1, y, final_norm[None], tm, interpret).reshape(B, S, D_MODEL)


def kernel(x, attn_norm, w_in, q_norm, w_uq, kv_norm, w_ukv, w_o_mla, w_o_moba, w_out, ffn_norm,
           w_router_group, b_router_group, w_router_expert, b_router_expert, w_exp_gate, w_exp_up,
           w_exp_down, final_norm):
    return _forward(x, attn_norm, w_in, q_norm, w_uq, kv_norm, w_ukv, w_o_mla, w_o_moba, w_out, ffn_norm,
                    w_router_group, b_router_group, w_router_expert, b_router_expert, w_exp_gate, w_exp_up,
                    w_exp_down, final_norm)
```

```python
import functools

import jax
import jax.numpy as jnp
import numpy as np
from jax import lax
from jax.experimental import pallas as pl
from jax.experimental.pallas import tpu as pltpu

F32 = jnp.float32
BF16 = jnp.bfloat16

D_MODEL = 1024
ROPE_THETA = 500000.0
NORM_EPS = 1e-6
MLA_HEADS = 8
MLA_NOPE = 64
MLA_ROPE = 32
MLA_V = 64
KV_LORA = 256
Q_LORA = 768
MOBA_HEADS = 8
MOBA_HD = 64
MOBA_ROT = 16
MOBA_BLOCK = 256
MOBA_TOPK = 3
N_GROUPS = 4
EXPERTS_PER_GROUP = 8
N_EXPERTS = 32
EXPERT_TOPK = 2
EXPERT_FF = 256
EXPERT_BLOCK = 256

LANES = 128
ROW_SLABS = D_MODEL // LANES
HEAD_PAIRS = MLA_HEADS // 2
NEG_BIG = -1e30
LOG2_E = 1.4426950408889634
VMEM_LIMIT = 56 * 1024 * 1024

SEG_CQ = (0, 768)
SEG_CKV = (768, 1024)
SEG_KR = (1024, 1152)
SEG_QB = (1152, 1664)
SEG_KB = (1664, 2176)
SEG_VB = (2176, 2688)
SEG_GA = (2688, 3712)
SEG_GB = (3712, 4736)
IN_WIDTH_R = 4736


def _dot(a, b):
    return jnp.dot(a, b, preferred_element_type=F32)


def _dot_nt(a, b):
    return lax.dot_general(a, b, (((1,), (1,)), ((), ())), preferred_element_type=F32)


def _rms(x, g):
    return x * lax.rsqrt(jnp.mean(x * x, axis=-1, keepdims=True) + NORM_EPS) * g


def _rope_tile(x, c, s_lo, s_hi, half):
    return x * c + pltpu.roll(x, LANES - half, 1) * s_lo + pltpu.roll(x, half, 1) * s_hi


def _proj_body(x_ref, g_ref, win_ref, qn_ref, wuq_ref, kvn_ref, wuk_ref, wuv_ref,
               mc_ref, ms1_ref, ms2_ref, bc_ref, bs1_ref, bs2_ref,
               qm_ref, km_ref, vm_ref, qb_ref, kb_ref, vb_ref, kmean_ref, ga_ref, gb_ref):
    tm = x_ref.shape[0]
    h = _rms(x_ref[...], g_ref[...]).astype(BF16)

    def seg(s):
        return _dot(h, win_ref[:, s[0]:s[1]])

    mc, ms1, ms2 = mc_ref[...], ms1_ref[...], ms2_ref[...]
    bc, bs1, bs2 = bc_ref[...], bs1_ref[...], bs2_ref[...]

    hq = _rms(seg(SEG_CQ), qn_ref[...]).astype(BF16)
    q = _dot(hq, wuq_ref[...])
    q_scale = (MLA_NOPE + MLA_ROPE) ** -0.5 * LOG2_E
    for hd in range(MLA_HEADS):
        sl = slice(hd * LANES, (hd + 1) * LANES)
        qm_ref[:, sl] = (_rope_tile(q[:, sl], mc, ms1, ms2, MLA_ROPE // 2) * q_scale).astype(BF16)

    hkv = _rms(seg(SEG_CKV), kvn_ref[...]).astype(BF16)
    k_nope = _dot(hkv, wuk_ref[...])
    kr = _rope_tile(seg(SEG_KR), mc, ms1, ms2, MLA_ROPE // 2)
    for hd in range(MLA_HEADS):
        sl = slice(hd * LANES, (hd + 1) * LANES)
        km_ref[:, sl] = (k_nope[:, sl] + kr).astype(BF16)
    vm_ref[...] = _dot(hkv, wuv_ref[...]).astype(BF16)

    qb = seg(SEG_QB)
    kb = seg(SEG_KB)
    qb_scale = MOBA_HD ** -0.5 * LOG2_E
    n_blk = tm // MOBA_BLOCK
    for t in range(MOBA_HEADS * MOBA_HD // LANES):
        sl = slice(t * LANES, (t + 1) * LANES)
        qb_ref[:, sl] = (_rope_tile(qb[:, sl], bc, bs1, bs2, MOBA_ROT // 2) * qb_scale).astype(BF16)
        kt = _rope_tile(kb[:, sl], bc, bs1, bs2, MOBA_ROT // 2)
        kb_ref[:, sl] = kt.astype(BF16)
        for j in range(n_blk):
            kmean_ref[j, :, sl] = jnp.mean(kt[j * MOBA_BLOCK:(j + 1) * MOBA_BLOCK], axis=0, keepdims=True)
    vb_ref[...] = seg(SEG_VB).astype(BF16)
    ga_ref[...] = seg(SEG_GA).astype(BF16)
    gb_ref[...] = seg(SEG_GB).astype(BF16)


def _proj_call(x2d, g_attn, w_in_r, q_norm, w_uq_r, kv_norm, w_uk_r, w_uv_r, tabs, seq, tm, interpret):
    T = x2d.shape[0]
    n_seq_tiles = seq // tm
    row = lambda i: (i, 0)
    const = lambda i: (0, 0)
    tab = lambda i: (i % n_seq_tiles, 0)
    full = lambda a: pl.BlockSpec(a.shape, const)
    n_blk = tm // MOBA_BLOCK
    out_shapes = [
        jax.ShapeDtypeStruct((T, MLA_HEADS * LANES), BF16),
        jax.ShapeDtypeStruct((T, MLA_HEADS * LANES), BF16),
        jax.ShapeDtypeStruct((T, MLA_HEADS * MLA_V), BF16),
        jax.ShapeDtypeStruct((T, 512), BF16),
        jax.ShapeDtypeStruct((T, 512), BF16),
        jax.ShapeDtypeStruct((T, 512), BF16),
        jax.ShapeDtypeStruct((T // MOBA_BLOCK, 1, 512), F32),
        jax.ShapeDtypeStruct((T, D_MODEL), BF16),
        jax.ShapeDtypeStruct((T, D_MODEL), BF16),
    ]
    out_specs = [
        pl.BlockSpec((tm, 1024), row), pl.BlockSpec((tm, 1024), row), pl.BlockSpec((tm, 512), row),
        pl.BlockSpec((tm, 512), row), pl.BlockSpec((tm, 512), row), pl.BlockSpec((tm, 512), row),
        pl.BlockSpec((n_blk, 1, 512), lambda i: (i, 0, 0)),
        pl.BlockSpec((tm, 1024), row), pl.BlockSpec((tm, 1024), row),
    ]
    in_specs = [pl.BlockSpec((tm, D_MODEL), row), full(g_attn), full(w_in_r), full(q_norm), full(w_uq_r),
                full(kv_norm), full(w_uk_r), full(w_uv_r)] + [pl.BlockSpec((tm, LANES), tab)] * 6
    return pl.pallas_call(
        _proj_body, grid=(T // tm,), in_specs=in_specs, out_specs=out_specs, out_shape=out_shapes,
        compiler_params=pltpu.CompilerParams(dimension_semantics=("parallel",), vmem_limit_bytes=VMEM_LIMIT),
        name="proj", interpret=interpret,
    )(x2d, g_attn, w_in_r, q_norm, w_uq_r, kv_norm, w_uk_r, w_uv_r, *tabs)


def _softmax_step(s, m, l):
    m_new = jnp.maximum(m, jnp.max(s, axis=-1, keepdims=True))
    alpha = jnp.exp2(m - m_new)
    p = jnp.exp2(s - m_new)
    return m_new, alpha, alpha * l + jnp.sum(p, axis=-1, keepdims=True), p.astype(BF16)


def _pair_update(state, s0, s1, v, lo):
    m0, l0, m1, l1, acc = state
    m0, a0, l0, p0 = _softmax_step(s0, m0, l0)
    m1, a1, l1, p1 = _softmax_step(s1, m1, l1)
    acc = acc * jnp.where(lo, a0, a1) + jnp.where(lo, _dot(p0, v), _dot(p1, v))
    return m0, l0, m1, l1, acc


def _init_state(tq):
    col = lambda val: jnp.full((tq, 1), val, F32)
    return col(-jnp.inf), col(0.0), col(-jnp.inf), col(0.0), jnp.zeros((tq, LANES), F32)


def _mla_body(q0_ref, q1_ref, k0_ref, k1_ref, v_ref, o_ref, *, tk):
    tq = q0_ref.shape[0]
    qi = pl.program_id(2)
    lo = lax.broadcasted_iota(jnp.int32, (1, LANES), 1) < MLA_V
    q0, q1 = q0_ref[...], q1_ref[...]

    def scores(start):
        rows = pl.ds(start, tk)
        return _dot_nt(q0, k0_ref[rows, :]), _dot_nt(q1, k1_ref[rows, :]), v_ref[rows, :]

    def past(j, state):
        s0, s1, v = scores(pl.multiple_of(j * tk, tk))
        return _pair_update(state, s0, s1, v, lo)

    n_full = (qi * tq) // tk
    state = lax.fori_loop(0, n_full, past, _init_state(tq))
    s0, s1, v = scores(pl.multiple_of(n_full * tk, tk))
    kpos = n_full * tk + lax.broadcasted_iota(jnp.int32, (tq, tk), 1)
    causal = kpos <= qi * tq + lax.broadcasted_iota(jnp.int32, (tq, tk), 0)
    s0 = jnp.where(causal, s0, -jnp.inf)
    s1 = jnp.where(causal, s1, -jnp.inf)
    _, l0, _, l1, acc = _pair_update(state, s0, s1, v, lo)
    o_ref[...] = (acc / jnp.where(lo, l0, l1)).astype(o_ref.dtype)


def _mla_call(q_mla, k_mla, v_mla, batch, seq, tq, tk, interpret):
    T = q_mla.shape[0]
    nq = seq // tq
    qspec = lambda off: pl.BlockSpec((tq, LANES), lambda b, p, i: (b * nq + i, 2 * p + off))
    kspec = lambda off: pl.BlockSpec((seq, LANES), lambda b, p, i: (b, 2 * p + off))
    return pl.pallas_call(
        functools.partial(_mla_body, tk=tk), grid=(batch, HEAD_PAIRS, nq),
        in_specs=[qspec(0), qspec(1), kspec(0), kspec(1), pl.BlockSpec((seq, LANES), lambda b, p, i: (b, p))],
        out_specs=pl.BlockSpec((tq, LANES), lambda b, p, i: (b * nq + i, p)),
        out_shape=jax.ShapeDtypeStruct((T, MLA_HEADS * MLA_V), BF16),
        compiler_params=pltpu.CompilerParams(dimension_semantics=("parallel", "parallel", "arbitrary"),
                                             vmem_limit_bytes=VMEM_LIMIT),
        name="mla_attn", interpret=interpret,
    )(q_mla, q_mla, k_mla, k_mla, v_mla)


def _top_blocks_bias(gate_t, n_past):
    blk = lax.broadcasted_iota(jnp.int32, gate_t.shape, 0)
    g = jnp.where(blk < n_past, gate_t, -jnp.inf)
    sel = blk == n_past
    for _ in range(MOBA_TOPK):
        mx = jnp.max(g, axis=0, keepdims=True)
        cand = (g == mx) & (mx > -jnp.inf)
        first = jnp.min(jnp.where(cand, blk, LANES), axis=0, keepdims=True)
        pick = blk == first
        sel = sel | pick
        g = jnp.where(pick, -jnp.inf, g)
    return jnp.where(sel, 0.0, NEG_BIG)


def _moba_body(q_ref, k_ref, v_ref, kmean_ref, blk_ref, o_ref, *, group):
    tq = q_ref.shape[0]
    qi = pl.program_id(2)
    lo = lax.broadcasted_iota(jnp.int32, (1, LANES), 1) < MOBA_HD
    q = q_ref[...]
    zq = jnp.zeros_like(q)
    kmean = kmean_ref[0]
    n_blocks = kmean.shape[0]

    def aug(qh):
        gate_t = lax.dot_general(kmean, qh.astype(F32), (((1,), (1,)), ((), ())),
                                 precision=lax.Precision.HIGHEST, preferred_element_type=F32)
        bias_t = jnp.concatenate([_top_blocks_bias(gate_t, qi), jnp.zeros((LANES - n_blocks, tq), F32)], axis=0)
        return jnp.concatenate([qh, bias_t.T.astype(BF16)], axis=1)

    qa0, qa1 = aug(jnp.where(lo, q, zq)), aug(jnp.where(lo, zq, q))
    tk = group * tq

    def scores(j):
        rows = pl.ds(pl.multiple_of(j * tk, tk), tk)
        k_aug = jnp.concatenate([k_ref[rows, :], blk_ref[rows, :]], axis=1)
        return _dot_nt(qa0, k_aug), _dot_nt(qa1, k_aug), v_ref[rows, :]

    def past(j, state):
        s0, s1, v = scores(j)
        return _pair_update(state, s0, s1, v, lo)

    j_own = qi // group
    state = lax.fori_loop(0, j_own, past, _init_state(tq))
    s0, s1, v = scores(j_own)
    kpos = j_own * tk + lax.broadcasted_iota(jnp.int32, (tq, tk), 1)
    causal = kpos <= qi * tq + lax.broadcasted_iota(jnp.int32, (tq, tk), 0)
    s0 = jnp.where(causal, s0, -jnp.inf)
    s1 = jnp.where(causal, s1, -jnp.inf)
    _, l0, _, l1, acc = _pair_update(state, s0, s1, v, lo)
    o_ref[...] = (acc / jnp.where(lo, l0, l1)).astype(o_ref.dtype)


def _moba_call(q_b, k_b, v_b, kmean, batch, seq, group, interpret):
    T = q_b.shape[0]
    tq = MOBA_BLOCK
    nq = seq // tq
    kv = pl.BlockSpec((seq, LANES), lambda b, p, i: (b, p))
    blk_onehot = (jnp.arange(seq, dtype=jnp.int32)[:, None] // MOBA_BLOCK
                  == jnp.arange(LANES, dtype=jnp.int32)[None, :]).astype(BF16)
    return pl.pallas_call(
        functools.partial(_moba_body, group=group), grid=(batch, HEAD_PAIRS, nq),
        in_specs=[pl.BlockSpec((tq, LANES), lambda b, p, i: (b * nq + i, p)), kv, kv,
                  pl.BlockSpec((1, nq, LANES), lambda b, p, i: (b, 0, p)),
                  pl.BlockSpec((seq, LANES), lambda b, p, i: (0, 0))],
        out_specs=pl.BlockSpec((tq, LANES), lambda b, p, i: (b * nq + i, p)),
        out_shape=jax.ShapeDtypeStruct((T, MOBA_HEADS * MOBA_HD), BF16),
        compiler_params=pltpu.CompilerParams(dimension_semantics=("parallel", "parallel", "arbitrary"),
                                             vmem_limit_bytes=VMEM_LIMIT),
        name="moba_attn", interpret=interpret,
    )(q_b, k_b, v_b, kmean, blk_onehot)


ROUTE_GROUP0 = 0
ROUTE_EXPERT0 = N_GROUPS


def _first_lane_of_max(vals, lane):
    mx = jnp.max(vals, axis=-1, keepdims=True)
    return mx, jnp.min(jnp.where(vals == mx, lane, LANES), axis=-1, keepdims=True)


def _mix_body(oa_ref, ob_ref, ga_ref, gb_ref, x_ref, woa_ref, wob_ref, wout_ref, fn_ref, wr_ref, br_ref,
              x1_ref, h2_ref, route_ref, counts_ref, carry_ref):
    tm = x_ref.shape[0]

    @pl.when(pl.program_id(0) == 0)
    def _():
        carry_ref[...] = jnp.zeros_like(carry_ref)

    a = _dot(oa_ref[...], woa_ref[...])
    b = _dot(ob_ref[...], wob_ref[...])
    mixed = jax.nn.sigmoid(ga_ref[...].astype(F32)) * a + jax.nn.sigmoid(gb_ref[...].astype(F32)) * b
    x1 = x_ref[...] + _dot(mixed.astype(BF16), wout_ref[...])
    x1_ref[...] = x1
    h2 = _rms(x1, fn_ref[...])
    for s in range(ROW_SLABS):
        h2_ref[:, s, :] = h2[:, s * LANES:(s + 1) * LANES]

    logits = jnp.dot(h2, wr_ref[...], precision=lax.Precision.HIGHEST, preferred_element_type=F32) + br_ref[...]
    lane = lax.broadcasted_iota(jnp.int32, (tm, LANES), 1)
    neg = -jnp.inf
    g_log = jnp.where(lane < N_GROUPS, logits, neg)
    g_max, g_sel = _first_lane_of_max(g_log, lane)
    p_group = 1.0 / jnp.sum(jnp.exp(g_log - g_max), axis=-1, keepdims=True)
    e_lo = ROUTE_EXPERT0 + g_sel * EXPERTS_PER_GROUP
    e_log = jnp.where((lane >= e_lo) & (lane < e_lo + EXPERTS_PER_GROUP), logits, neg)
    v1, i1 = _first_lane_of_max(e_log, lane)
    v2, i2 = _first_lane_of_max(jnp.where(lane == i1, neg, e_log), lane)
    e2 = jnp.exp(v2 - v1)
    w1 = p_group * (1.0 / (1.0 + e2))
    w2 = p_group * (e2 / (1.0 + e2))

    pick1, pick2 = lane == i1, lane == i2
    onehot = (pick1 | pick2).astype(F32)
    tri = (lax.broadcasted_iota(jnp.int32, (tm, tm), 1) < lax.broadcasted_iota(jnp.int32, (tm, tm), 0))
    before = _dot(tri.astype(BF16), onehot.astype(BF16)) + carry_ref[0:1, :]
    r1 = jnp.sum(jnp.where(pick1, before, 0.0), axis=-1, keepdims=True)
    r2 = jnp.sum(jnp.where(pick2, before, 0.0), axis=-1, keepdims=True)
    new_carry = carry_ref[0:1, :] + jnp.sum(onehot, axis=0, keepdims=True)
    carry_ref[...] = jnp.broadcast_to(new_carry, carry_ref.shape)
    counts_ref[...] = jnp.broadcast_to(new_carry, counts_ref.shape)

    fi1 = (i1 - ROUTE_EXPERT0).astype(F32)
    fi2 = (i2 - ROUTE_EXPERT0).astype(F32)
    route = jnp.zeros((tm, LANES), F32)
    for k, val in enumerate((fi1, fi2, w1, w2, r1, r2)):
        route = jnp.where(lane == k, val, route)
    route_ref[...] = route


def _mix_call(o_a, o_b, g_a, g_b, x2d, w_oa, w_ob, w_out, ffn_norm, w_router, b_router, tm, interpret):
    T = x2d.shape[0]
    row = lambda i: (i, 0)
    const = lambda i: (0, 0)
    full = lambda a: pl.BlockSpec(a.shape, const)
    return pl.pallas_call(
        _mix_body, grid=(T // tm,),
        in_specs=[pl.BlockSpec((tm, 512), row), pl.BlockSpec((tm, 512), row),
                  pl.BlockSpec((tm, D_MODEL), row), pl.BlockSpec((tm, D_MODEL), row),
                  pl.BlockSpec((tm, D_MODEL), row), full(w_oa), full(w_ob), full(w_out), full(ffn_norm),
                  full(w_router), full(b_router)],
        out_specs=[pl.BlockSpec((tm, D_MODEL), row), pl.BlockSpec((tm, ROW_SLABS, LANES), lambda i: (i, 0, 0)),
                   pl.BlockSpec((tm, LANES), row), pl.BlockSpec((8, LANES), const)],
        out_shape=[jax.ShapeDtypeStruct((T, D_MODEL), F32), jax.ShapeDtypeStruct((T, ROW_SLABS, LANES), F32),
                   jax.ShapeDtypeStruct((T, LANES), F32), jax.ShapeDtypeStruct((8, LANES), F32)],
        scratch_shapes=[pltpu.VMEM((8, LANES), F32)],
        compiler_params=pltpu.CompilerParams(dimension_semantics=("arbitrary",), vmem_limit_bytes=VMEM_LIMIT),
        name="mix_route", interpret=interpret,
    )(o_a, o_b, g_a, g_b, x2d, w_oa, w_ob, w_out, ffn_norm, w_router, b_router)


def _slab_rows(ref):
    return jnp.concatenate([ref[:, s, :] for s in range(ROW_SLABS)], axis=1)


def _gather_rows(src_hbm, idx_ref, n, dst_ref, sem):
    def issue(r, c):
        pltpu.make_async_copy(src_hbm.at[idx_ref[0, 0, r]], dst_ref.at[r], sem).start()
        return c

    lax.fori_loop(0, n, issue, 0)
    pltpu.make_async_copy(src_hbm.at[pl.ds(0, n)], dst_ref, sem).wait()


def _moe_body(blk_e_ref, n_used_ref, tok_ref, w_ref, h_hbm, wg_ref, wu_ref, wd_ref, y_ref, xs_ref, sem):
    del blk_e_ref
    nb = xs_ref.shape[0]
    used = pl.program_id(0) < n_used_ref[0]

    @pl.when(used)
    def _():
        _gather_rows(h_hbm, tok_ref, nb, xs_ref, sem.at[0])
        xs = _slab_rows(xs_ref).astype(BF16)
        act = jax.nn.silu(_dot(xs, wg_ref[0])) * _dot(xs, wu_ref[0])
        ys = _dot(act.astype(BF16), wd_ref[0]) * w_ref[...]
        for s in range(ROW_SLABS):
            y_ref[:, s, :] = ys[:, s * LANES:(s + 1) * LANES]

    @pl.when(jnp.logical_not(used))
    def _():
        y_ref[...] = jnp.zeros_like(y_ref)


def _moe_call(blk_e, n_used, slot_tok, slot_w, h2, w_g, w_u, w_d, interpret):
    n_blocks = slot_tok.shape[0]
    nb = EXPERT_BLOCK
    grid_spec = pltpu.PrefetchScalarGridSpec(
        num_scalar_prefetch=2, grid=(n_blocks,),
        in_specs=[pl.BlockSpec((1, 1, nb), lambda i, be, nu: (i, 0, 0), memory_space=pltpu.SMEM),
                  pl.BlockSpec((nb, 1), lambda i, be, nu: (i, 0)),
                  pl.BlockSpec(memory_space=pl.ANY),
                  pl.BlockSpec((1, D_MODEL, EXPERT_FF), lambda i, be, nu: (be[i], 0, 0)),
                  pl.BlockSpec((1, D_MODEL, EXPERT_FF), lambda i, be, nu: (be[i], 0, 0)),
                  pl.BlockSpec((1, EXPERT_FF, D_MODEL), lambda i, be, nu: (be[i], 0, 0))],
        out_specs=pl.BlockSpec((nb, ROW_SLABS, LANES), lambda i, be, nu: (i, 0, 0)),
        scratch_shapes=[pltpu.VMEM((nb, ROW_SLABS, LANES), F32), pltpu.SemaphoreType.DMA((1,))],
    )
    return pl.pallas_call(
        _moe_body, grid_spec=grid_spec,
        out_shape=jax.ShapeDtypeStruct((n_blocks * nb, ROW_SLABS, LANES), F32),
        compiler_params=pltpu.CompilerParams(dimension_semantics=("arbitrary",), vmem_limit_bytes=VMEM_LIMIT),
        name="moe_experts", interpret=interpret,
    )(blk_e, n_used, slot_tok, slot_w, h2, w_g, w_u, w_d)


def _final_body(dest_ref, x1_ref, y_hbm, g_ref, o_ref, ybuf_ref, sem):
    tm = x1_ref.shape[0]
    _gather_rows(y_hbm, dest_ref, EXPERT_TOPK * tm, ybuf_ref, sem.at[0])
    y = x1_ref[...] + _slab_rows(ybuf_ref.at[pl.ds(0, tm)]) + _slab_rows(ybuf_ref.at[pl.ds(tm, tm)])
    o_ref[...] = _rms(y, g_ref[...])


def _final_call(dest, x1, y, final_norm, tm, interpret):
    T = x1.shape[0]
    return pl.pallas_call(
        _final_body, grid=(T // tm,),
        in_specs=[pl.BlockSpec((1, 1, EXPERT_TOPK * tm), lambda i: (i, 0, 0), memory_space=pltpu.SMEM),
                  pl.BlockSpec((tm, D_MODEL), lambda i: (i, 0)),
                  pl.BlockSpec(memory_space=pl.ANY),
                  pl.BlockSpec((1, D_MODEL), lambda i: (0, 0))],
        out_specs=pl.BlockSpec((tm, D_MODEL), lambda i: (i, 0)),
        out_shape=jax.ShapeDtypeStruct((T, D_MODEL), F32),
        scratch_shapes=[pltpu.VMEM((EXPERT_TOPK * tm, ROW_SLABS, LANES), F32), pltpu.SemaphoreType.DMA((1,))],
        compiler_params=pltpu.CompilerParams(dimension_semantics=("arbitrary",), vmem_limit_bytes=VMEM_LIMIT),
        name="final_norm", interpret=interpret,
    )(dest, x1, y, final_norm)


def _rope_tables(seq, rot_dim):
    half = rot_dim // 2
    inv_freq = jnp.power(ROPE_THETA, -jnp.arange(half, dtype=F32) / half)
    ang = jnp.arange(seq, dtype=F32)[:, None] * inv_freq[None, :]
    return jnp.cos(ang), jnp.sin(ang)


def _lane_tables(seq):
    cos, sin = _rope_tables(seq, MLA_ROPE)
    z = lambda n: jnp.zeros((seq, n), F32)
    o = lambda n: jnp.ones((seq, n), F32)
    mla = (jnp.concatenate([o(64), cos, cos, o(32)], 1),
           jnp.concatenate([z(64), -sin, z(48)], 1),
           jnp.concatenate([z(80), sin, z(32)], 1))
    cos, sin = _rope_tables(seq, MOBA_ROT)
    head = (jnp.concatenate([cos, cos, o(48)], 1), jnp.concatenate([-sin, z(56)], 1),
            jnp.concatenate([z(8), sin, z(48)], 1))
    moba = tuple(jnp.concatenate([t, t], 1) for t in head)
    return mla + moba


def _prep_weights(w_in, w_uq, w_ukv):
    kr = jnp.pad(w_in[:, 1024:1056], ((0, 0), (64, 32)))
    w_in_r = jnp.concatenate([w_in[:, :1024], kr, w_in[:, 1056:]], axis=1).astype(BF16)
    w_uq_r = jnp.pad(w_uq.reshape(Q_LORA, MLA_HEADS, MLA_NOPE + MLA_ROPE),
                     ((0, 0), (0, 0), (0, LANES - MLA_NOPE - MLA_ROPE))).reshape(Q_LORA, MLA_HEADS * LANES)
    w_kv = w_ukv.reshape(KV_LORA, MLA_HEADS, MLA_NOPE + MLA_V)
    w_uk_r = jnp.pad(w_kv[:, :, :MLA_NOPE], ((0, 0), (0, 0), (0, LANES - MLA_NOPE))).reshape(KV_LORA, -1)
    w_uv_r = w_kv[:, :, MLA_NOPE:].reshape(KV_LORA, MLA_HEADS * MLA_V)
    return w_in_r, w_uq_r.astype(BF16), w_uk_r.astype(BF16), w_uv_r.astype(BF16)


def _router_weights(w_rg, b_rg, w_re, b_re):
    w_e = jnp.transpose(w_re, (1, 0, 2)).reshape(D_MODEL, N_EXPERTS)
    w = jnp.concatenate([w_rg, w_e], axis=1)
    b = jnp.concatenate([b_rg, b_re.reshape(N_EXPERTS)])
    pad = LANES - N_GROUPS - N_EXPERTS
    return jnp.pad(w, ((0, 0), (0, pad))), jnp.pad(b, (0, pad))[None, :]


def _dispatch(route, counts, T, tm):
    eid = route[:, 0:2].astype(jnp.int32)
    w = route[:, 2:4]
    rank = route[:, 4:6].astype(jnp.int32)
    cnt = counts[0, ROUTE_EXPERT0:ROUTE_EXPERT0 + N_EXPERTS].astype(jnp.int32)
    padded = (cnt + EXPERT_BLOCK - 1) // EXPERT_BLOCK * EXPERT_BLOCK
    pend = jnp.cumsum(padded)
    pstart = pend - padded
    dest = pstart[eid] + rank
    n_blocks = T * EXPERT_TOPK // EXPERT_BLOCK + N_EXPERTS
    n_slots = n_blocks * EXPERT_BLOCK
    tok = jnp.repeat(jnp.arange(T, dtype=jnp.int32), EXPERT_TOPK)
    slot_tok = jnp.zeros((n_slots,), jnp.int32).at[dest.reshape(-1)].set(tok)
    slot_w = jnp.zeros((n_slots,), F32).at[dest.reshape(-1)].set(w.reshape(-1))
    blk_e = jnp.minimum(jnp.searchsorted(pend, jnp.arange(n_blocks, dtype=jnp.int32) * EXPERT_BLOCK, side='right'),
                        N_EXPERTS - 1).astype(jnp.int32)
    n_used = (pend[-1] // EXPERT_BLOCK).astype(jnp.int32)[None]
    dest_tiles = dest.reshape(T // tm, tm, EXPERT_TOPK).transpose(0, 2, 1).reshape(T // tm, 1, EXPERT_TOPK * tm)
    return blk_e, n_used, slot_tok.reshape(n_blocks, 1, EXPERT_BLOCK), slot_w[:, None], dest_tiles


def _forward(x, attn_norm, w_in, q_norm, w_uq, kv_norm, w_ukv, w_o_mla, w_o_moba, w_out, ffn_norm,
             w_router_group, b_router_group, w_router_expert, b_router_expert, w_exp_gate, w_exp_up,
             w_exp_down, final_norm, *, tm=256, tq_mla=512, tk_mla=1024, moba_group=4, interpret=False):
    B, S, _ = x.shape
    T = B * S
    x2d = x.reshape(T, D_MODEL)
    tabs = _lane_tables(S)
    for l in range(attn_norm.shape[0]):
        w_in_r, w_uq_r, w_uk_r, w_uv_r = _prep_weights(w_in[l], w_uq[l], w_ukv[l])
        q_m, k_m, v_m, q_b, k_b, v_b, kmean, g_a, g_b = _proj_call(
            x2d, attn_norm[l][None], w_in_r, q_norm[l][None], w_uq_r, kv_norm[l][None], w_uk_r, w_uv_r,
            tabs, S, tm, interpret)
        o_a = _mla_call(q_m, k_m, v_m, B, S, tq_mla, tk_mla, interpret)
        o_b = _moba_call(q_b, k_b, v_b, kmean.reshape(B, S // MOBA_BLOCK, 512), B, S, moba_group, interpret)
        w_r, b_r = _router_weights(w_router_group[l], b_router_group[l], w_router_expert[l], b_router_expert[l])
        x1, h2, route, counts = _mix_call(o_a, o_b, g_a, g_b, x2d, w_o_mla[l].astype(BF16),
                                          w_o_moba[l].astype(BF16), w_out[l].astype(BF16), ffn_norm[l][None],
                                          w_r, b_r, tm, interpret)
        blk_e, n_used, slot_tok, slot_w, dest_tiles = _dispatch(route, counts, T, tm)
        y = _moe_call(blk_e, n_used, slot_tok, slot_w, h2, w_exp_gate[l].astype(BF16),
                      w_exp_up[l].astype(BF16), w_exp_down[l].astype(BF16), interpret)
        assert attn_norm.shape[0] == 1
    return _final_call(dest_tiles, x1, y, final_norm[None], tm, interpret).reshape(B, S, D_MODEL)


def kernel(x, attn_norm, w_in, q_norm, w_uq, kv_norm, w_ukv, w_o_mla, w_o_moba, w_out, ffn_norm,
           w_router_group, b_router_group, w_router_expert, b_router_expert, w_exp_gate, w_exp_up,
           w_exp_down, final_norm):
    return _forward(x, attn_norm, w_in, q_norm, w_uq, kv_norm, w_ukv, w_o_mla, w_o_moba, w_out, ffn_norm,
                    w_router_group, b_router_group, w_router_expert, b_router_expert, w_exp_gate, w_exp_up,
                    w_exp_down, final_norm)
```

```python
import functools

import jax
import jax.numpy as jnp
import numpy as np
from jax import lax
from jax.experimental import pallas as pl
from jax.experimental.pallas import tpu as pltpu

F32 = jnp.float32
BF16 = jnp.bfloat16

D_MODEL = 1024
ROPE_THETA = 500000.0
NORM_EPS = 1e-6
MLA_HEADS = 8
MLA_NOPE = 64
MLA_ROPE = 32
MLA_V = 64
KV_LORA = 256
Q_LORA = 768
MOBA_HEADS = 8
MOBA_HD = 64
MOBA_ROT = 16
MOBA_BLOCK = 256
MOBA_TOPK = 3
N_GROUPS = 4
EXPERTS_PER_GROUP = 8
N_EXPERTS = 32
EXPERT_TOPK = 2
EXPERT_FF = 256
EXPERT_BLOCK = 256

LANES = 128
ROW_SLABS = D_MODEL // LANES
HEAD_PAIRS = MLA_HEADS // 2
NEG_BIG = -1e30
LOG2_E = 1.4426950408889634
VMEM_LIMIT = 56 * 1024 * 1024

SEG_CQ = (0, 768)
SEG_CKV = (768, 1024)
SEG_KR = (1024, 1152)
SEG_QB = (1152, 1664)
SEG_KB = (1664, 2176)
SEG_VB = (2176, 2688)
SEG_GA = (2688, 3712)
SEG_GB = (3712, 4736)
IN_WIDTH_R = 4736


def _dot(a, b):
    return jnp.dot(a, b, preferred_element_type=F32)


def _dot_nt(a, b):
    return lax.dot_general(a, b, (((1,), (1,)), ((), ())), preferred_element_type=F32)


def _rms(x, g):
    return x * lax.rsqrt(jnp.mean(x * x, axis=-1, keepdims=True) + NORM_EPS) * g


def _rope_tile(x, c, s_lo, s_hi, half):
    return x * c + pltpu.roll(x, LANES - half, 1) * s_lo + pltpu.roll(x, half, 1) * s_hi


def _proj_body(x_ref, g_ref, win_ref, qn_ref, wuq_ref, kvn_ref, wuk_ref, wuv_ref,
               mc_ref, ms1_ref, ms2_ref, bc_ref, bs1_ref, bs2_ref,
               qm_ref, km_ref, vm_ref, qb_ref, kb_ref, vb_ref, kmean_ref, ga_ref, gb_ref):
    tm = x_ref.shape[0]
    h = _rms(x_ref[...], g_ref[...]).astype(BF16)

    def seg(s):
        return _dot(h, win_ref[:, s[0]:s[1]])

    mc, ms1, ms2 = mc_ref[...], ms1_ref[...], ms2_ref[...]
    bc, bs1, bs2 = bc_ref[...], bs1_ref[...], bs2_ref[...]

    hq = _rms(seg(SEG_CQ), qn_ref[...]).astype(BF16)
    q = _dot(hq, wuq_ref[...])
    q_scale = (MLA_NOPE + MLA_ROPE) ** -0.5 * LOG2_E
    for hd in range(MLA_HEADS):
        sl = slice(hd * LANES, (hd + 1) * LANES)
        qm_ref[:, sl] = (_rope_tile(q[:, sl], mc, ms1, ms2, MLA_ROPE // 2) * q_scale).astype(BF16)

    hkv = _rms(seg(SEG_CKV), kvn_ref[...]).astype(BF16)
    k_nope = _dot(hkv, wuk_ref[...])
    kr = _rope_tile(seg(SEG_KR), mc, ms1, ms2, MLA_ROPE // 2)
    for hd in range(MLA_HEADS):
        sl = slice(hd * LANES, (hd + 1) * LANES)
        km_ref[:, sl] = (k_nope[:, sl] + kr).astype(BF16)
    vm_ref[...] = _dot(hkv, wuv_ref[...]).astype(BF16)

    qb = seg(SEG_QB)
    kb = seg(SEG_KB)
    qb_scale = MOBA_HD ** -0.5 * LOG2_E
    n_blk = tm // MOBA_BLOCK
    for t in range(MOBA_HEADS * MOBA_HD // LANES):
        sl = slice(t * LANES, (t + 1) * LANES)
        qb_ref[:, sl] = (_rope_tile(qb[:, sl], bc, bs1, bs2, MOBA_ROT // 2) * qb_scale).astype(BF16)
        kt = _rope_tile(kb[:, sl], bc, bs1, bs2, MOBA_ROT // 2)
        kb_ref[:, sl] = kt.astype(BF16)
        for j in range(n_blk):
            kmean_ref[j, :, sl] = jnp.mean(kt[j * MOBA_BLOCK:(j + 1) * MOBA_BLOCK], axis=0, keepdims=True)
    vb_ref[...] = seg(SEG_VB).astype(BF16)
    ga_ref[...] = seg(SEG_GA).astype(BF16)
    gb_ref[...] = seg(SEG_GB).astype(BF16)


def _proj_call(x2d, g_attn, w_in_r, q_norm, w_uq_r, kv_norm, w_uk_r, w_uv_r, tabs, seq, tm, interpret):
    T = x2d.shape[0]
    n_seq_tiles = seq // tm
    row = lambda i: (i, 0)
    const = lambda i: (0, 0)
    tab = lambda i: (i % n_seq_tiles, 0)
    full = lambda a: pl.BlockSpec(a.shape, const, pipeline_mode=pl.Buffered(1))
    n_blk = tm // MOBA_BLOCK
    out_shapes = [
        jax.ShapeDtypeStruct((T, MLA_HEADS * LANES), BF16),
        jax.ShapeDtypeStruct((T, MLA_HEADS * LANES), BF16),
        jax.ShapeDtypeStruct((T, MLA_HEADS * MLA_V), BF16),
        jax.ShapeDtypeStruct((T, 512), BF16),
        jax.ShapeDtypeStruct((T, 512), BF16),
        jax.ShapeDtypeStruct((T, 512), BF16),
        jax.ShapeDtypeStruct((T // MOBA_BLOCK, 1, 512), F32),
        jax.ShapeDtypeStruct((T, D_MODEL), BF16),
        jax.ShapeDtypeStruct((T, D_MODEL), BF16),
    ]
    out_specs = [
        pl.BlockSpec((tm, 1024), row), pl.BlockSpec((tm, 1024), row), pl.BlockSpec((tm, 512), row),
        pl.BlockSpec((tm, 512), row), pl.BlockSpec((tm, 512), row), pl.BlockSpec((tm, 512), row),
        pl.BlockSpec((n_blk, 1, 512), lambda i: (i, 0, 0)),
        pl.BlockSpec((tm, 1024), row), pl.BlockSpec((tm, 1024), row),
    ]
    in_specs = [pl.BlockSpec((tm, D_MODEL), row), full(g_attn), full(w_in_r), full(q_norm), full(w_uq_r),
                full(kv_norm), full(w_uk_r), full(w_uv_r)] + [pl.BlockSpec((tm, LANES), tab)] * 6
    return pl.pallas_call(
        _proj_body, grid=(T // tm,), in_specs=in_specs, out_specs=out_specs, out_shape=out_shapes,
        compiler_params=pltpu.CompilerParams(dimension_semantics=("parallel",), vmem_limit_bytes=VMEM_LIMIT),
        name="proj", interpret=interpret,
    )(x2d, g_attn, w_in_r, q_norm, w_uq_r, kv_norm, w_uk_r, w_uv_r, *tabs)


def _softmax_step(s, m, l):
    m_new = jnp.maximum(m, jnp.max(s, axis=-1, keepdims=True))
    alpha = jnp.exp2(m - m_new)
    p = jnp.exp2(s - m_new)
    return m_new, alpha, alpha * l + jnp.sum(p, axis=-1, keepdims=True), p.astype(BF16)


def _pair_update(state, s0, s1, v, lo):
    m0, l0, m1, l1, acc = state
    m0, a0, l0, p0 = _softmax_step(s0, m0, l0)
    m1, a1, l1, p1 = _softmax_step(s1, m1, l1)
    acc = acc * jnp.where(lo, a0, a1) + jnp.where(lo, _dot(p0, v), _dot(p1, v))
    return m0, l0, m1, l1, acc


def _init_state(tq):
    col = lambda val: jnp.full((tq, 1), val, F32)
    return col(-jnp.inf), col(0.0), col(-jnp.inf), col(0.0), jnp.zeros((tq, LANES), F32)


def _pipelined_sweep(n_full, scores, values, s_ref, q_pos0, tq, tk, lo):
    def produce(j, slot):
        s0, s1 = scores(j)
        s_ref[slot, 0] = s0
        s_ref[slot, 1] = s1

    def consume(j, slot, state):
        return _pair_update(state, s_ref[slot, 0], s_ref[slot, 1], values(j), lo)

    def two_chunks(i, state):
        produce(2 * i + 1, 1)
        state = consume(2 * i, 0, state)
        produce(2 * i + 2, 0)
        return consume(2 * i + 1, 1, state)

    def odd_tail(state):
        produce(n_full, 1)
        return consume(n_full - 1, 0, state)

    produce(0, 0)
    state = lax.fori_loop(0, n_full // 2, two_chunks, _init_state(tq))
    state = lax.cond(n_full % 2 == 1, odd_tail, lambda st: st, state)
    slot = n_full % 2
    kpos = n_full * tk + lax.broadcasted_iota(jnp.int32, (tq, tk), 1)
    causal = kpos <= q_pos0 + lax.broadcasted_iota(jnp.int32, (tq, tk), 0)
    s0 = jnp.where(causal, s_ref[slot, 0], -jnp.inf)
    s1 = jnp.where(causal, s_ref[slot, 1], -jnp.inf)
    _, l0, _, l1, acc = _pair_update(state, s0, s1, values(n_full), lo)
    return acc / jnp.where(lo, l0, l1)


def _mla_body(q0_ref, q1_ref, k0_ref, k1_ref, v_ref, o_ref, s_ref, *, tk):
    tq = q0_ref.shape[0]
    qi = pl.program_id(2)
    lo = lax.broadcasted_iota(jnp.int32, (1, LANES), 1) < MLA_V
    q0, q1 = q0_ref[...], q1_ref[...]
    rows = lambda j: pl.ds(pl.multiple_of(j * tk, tk), tk)
    scores = lambda j: (_dot_nt(q0, k0_ref[rows(j), :]), _dot_nt(q1, k1_ref[rows(j), :]))
    values = lambda j: v_ref[rows(j), :]
    out = _pipelined_sweep((qi * tq) // tk, scores, values, s_ref, qi * tq, tq, tk, lo)
    o_ref[...] = out.astype(o_ref.dtype)


def _mla_call(q_mla, k_mla, v_mla, batch, seq, tq, tk, interpret):
    T = q_mla.shape[0]
    nq = seq // tq
    qspec = lambda off: pl.BlockSpec((tq, LANES), lambda b, p, i: (b * nq + i, 2 * p + off))
    kspec = lambda off: pl.BlockSpec((seq, LANES), lambda b, p, i: (b, 2 * p + off))
    return pl.pallas_call(
        functools.partial(_mla_body, tk=tk), grid=(batch, HEAD_PAIRS, nq),
        in_specs=[qspec(0), qspec(1), kspec(0), kspec(1), pl.BlockSpec((seq, LANES), lambda b, p, i: (b, p))],
        out_specs=pl.BlockSpec((tq, LANES), lambda b, p, i: (b * nq + i, p)),
        out_shape=jax.ShapeDtypeStruct((T, MLA_HEADS * MLA_V), BF16),
        scratch_shapes=[pltpu.VMEM((2, 2, tq, tk), F32)],
        compiler_params=pltpu.CompilerParams(dimension_semantics=("parallel", "parallel", "arbitrary"),
                                             vmem_limit_bytes=VMEM_LIMIT),
        name="mla_attn", interpret=interpret,
    )(q_mla, q_mla, k_mla, k_mla, v_mla)


def _top_blocks_bias(gate_t, n_past):
    blk = lax.broadcasted_iota(jnp.int32, gate_t.shape, 0)
    g = jnp.where(blk < n_past, gate_t, -jnp.inf)
    sel = blk == n_past
    for _ in range(MOBA_TOPK):
        mx = jnp.max(g, axis=0, keepdims=True)
        cand = (g == mx) & (mx > -jnp.inf)
        first = jnp.min(jnp.where(cand, blk, LANES), axis=0, keepdims=True)
        pick = blk == first
        sel = sel | pick
        g = jnp.where(pick, -jnp.inf, g)
    return jnp.where(sel, 0.0, NEG_BIG)


def _moba_body(q_ref, k_ref, v_ref, kmean_ref, blk_ref, o_ref, s_ref, *, group):
    tq = q_ref.shape[0]
    qi = pl.program_id(2)
    lo = lax.broadcasted_iota(jnp.int32, (1, LANES), 1) < MOBA_HD
    q = q_ref[...]
    zq = jnp.zeros_like(q)
    kmean = kmean_ref[0]
    n_blocks = kmean.shape[0]

    def aug(qh):
        gate_t = lax.dot_general(kmean, qh.astype(F32), (((1,), (1,)), ((), ())),
                                 precision=lax.Precision.HIGHEST, preferred_element_type=F32)
        bias_t = jnp.concatenate([_top_blocks_bias(gate_t, qi), jnp.zeros((LANES - n_blocks, tq), F32)], axis=0)
        return jnp.concatenate([qh, bias_t.T.astype(BF16)], axis=1)

    qa0, qa1 = aug(jnp.where(lo, q, zq)), aug(jnp.where(lo, zq, q))
    tk = group * tq

    rows = lambda j: pl.ds(pl.multiple_of(j * tk, tk), tk)

    def scores(j):
        k_aug = jnp.concatenate([k_ref[rows(j), :], blk_ref[rows(j), :]], axis=1)
        return _dot_nt(qa0, k_aug), _dot_nt(qa1, k_aug)

    out = _pipelined_sweep(qi // group, scores, lambda j: v_ref[rows(j), :], s_ref, qi * tq, tq, tk, lo)
    o_ref[...] = out.astype(o_ref.dtype)


def _moba_call(q_b, k_b, v_b, kmean, batch, seq, group, interpret):
    T = q_b.shape[0]
    tq = MOBA_BLOCK
    nq = seq // tq
    kv = pl.BlockSpec((seq, LANES), lambda b, p, i: (b, p))
    blk_onehot = (jnp.arange(seq, dtype=jnp.int32)[:, None] // MOBA_BLOCK
                  == jnp.arange(LANES, dtype=jnp.int32)[None, :]).astype(BF16)
    return pl.pallas_call(
        functools.partial(_moba_body, group=group), grid=(batch, HEAD_PAIRS, nq),
        in_specs=[pl.BlockSpec((tq, LANES), lambda b, p, i: (b * nq + i, p)), kv, kv,
                  pl.BlockSpec((1, nq, LANES), lambda b, p, i: (b, 0, p)),
                  pl.BlockSpec((seq, LANES), lambda b, p, i: (0, 0))],
        out_specs=pl.BlockSpec((tq, LANES), lambda b, p, i: (b * nq + i, p)),
        out_shape=jax.ShapeDtypeStruct((T, MOBA_HEADS * MOBA_HD), BF16),
        scratch_shapes=[pltpu.VMEM((2, 2, tq, group * tq), F32)],
        compiler_params=pltpu.CompilerParams(dimension_semantics=("parallel", "parallel", "arbitrary"),
                                             vmem_limit_bytes=VMEM_LIMIT),
        name="moba_attn", interpret=interpret,
    )(q_b, k_b, v_b, kmean, blk_onehot)


def _slab_rows(ref, row0, n):
    return jnp.concatenate([ref[pl.ds(row0 * ROW_SLABS + s, n, stride=ROW_SLABS), :] for s in range(ROW_SLABS)],
                           axis=1)


def _store_slab_rows(ref, val):
    for s in range(ROW_SLABS):
        ref[pl.ds(s, val.shape[0], stride=ROW_SLABS), :] = val[:, s * LANES:(s + 1) * LANES]


ROUTE_GROUP0 = 0
ROUTE_EXPERT0 = N_GROUPS
ROUTE_EID0, ROUTE_W0, ROUTE_RANK0 = 0, 2, 4


def _first_lane_of_max(vals, lane):
    mx = jnp.max(vals, axis=-1, keepdims=True)
    return mx, jnp.min(jnp.where(vals == mx, lane, LANES), axis=-1, keepdims=True)


def _mix_body(oa_ref, ob_ref, ga_ref, gb_ref, x_ref, woa_ref, wob_ref, wout_ref, fn_ref, wr_ref, br_ref,
              x1_ref, h2_ref, route_ref, counts_ref, carry_ref):
    tm = x_ref.shape[0]

    @pl.when(pl.program_id(0) == 0)
    def _():
        carry_ref[...] = jnp.zeros_like(carry_ref)

    a = _dot(oa_ref[...], woa_ref[...])
    b = _dot(ob_ref[...], wob_ref[...])
    mixed = jax.nn.sigmoid(ga_ref[...].astype(F32)) * a + jax.nn.sigmoid(gb_ref[...].astype(F32)) * b
    x1 = x_ref[...] + _dot(mixed.astype(BF16), wout_ref[...])
    x1_ref[...] = x1
    h2 = _rms(x1, fn_ref[...])
    _store_slab_rows(h2_ref, h2)

    logits = jnp.dot(h2, wr_ref[...], precision=lax.Precision.HIGHEST, preferred_element_type=F32) + br_ref[...]
    lane = lax.broadcasted_iota(jnp.int32, (tm, LANES), 1)
    neg = -jnp.inf
    g_log = jnp.where(lane < N_GROUPS, logits, neg)
    g_max, g_sel = _first_lane_of_max(g_log, lane)
    p_group = 1.0 / jnp.sum(jnp.exp(g_log - g_max), axis=-1, keepdims=True)
    e_lo = ROUTE_EXPERT0 + g_sel * EXPERTS_PER_GROUP
    e_log = jnp.where((lane >= e_lo) & (lane < e_lo + EXPERTS_PER_GROUP), logits, neg)
    v1, i1 = _first_lane_of_max(e_log, lane)
    v2, i2 = _first_lane_of_max(jnp.where(lane == i1, neg, e_log), lane)
    e2 = jnp.exp(v2 - v1)
    w1 = p_group * (1.0 / (1.0 + e2))
    w2 = p_group * (e2 / (1.0 + e2))

    pick1, pick2 = lane == i1, lane == i2
    onehot = (pick1 | pick2).astype(F32)
    tri = (lax.broadcasted_iota(jnp.int32, (tm, tm), 1) < lax.broadcasted_iota(jnp.int32, (tm, tm), 0))
    before = _dot(tri.astype(BF16), onehot.astype(BF16)) + carry_ref[0:1, :]
    r1 = jnp.sum(jnp.where(pick1, before, 0.0), axis=-1, keepdims=True)
    r2 = jnp.sum(jnp.where(pick2, before, 0.0), axis=-1, keepdims=True)
    new_carry = carry_ref[0:1, :] + jnp.sum(onehot, axis=0, keepdims=True)
    carry_ref[...] = jnp.broadcast_to(new_carry, carry_ref.shape)
    counts_ref[...] = jnp.broadcast_to(new_carry, counts_ref.shape)

    fi1 = (i1 - ROUTE_EXPERT0).astype(F32)
    fi2 = (i2 - ROUTE_EXPERT0).astype(F32)
    route = jnp.zeros((tm, LANES), F32)
    for k, val in ((ROUTE_EID0, fi1), (ROUTE_EID0 + 1, fi2), (ROUTE_W0, w1), (ROUTE_W0 + 1, w2),
                   (ROUTE_RANK0, r1), (ROUTE_RANK0 + 1, r2)):
        route = jnp.where(lane == k, val, route)
    route_ref[...] = route


def _mix_call(o_a, o_b, g_a, g_b, x2d, w_oa, w_ob, w_out, ffn_norm, w_router, b_router, tm, interpret):
    T = x2d.shape[0]
    row = lambda i: (i, 0)
    const = lambda i: (0, 0)
    full = lambda a: pl.BlockSpec(a.shape, const, pipeline_mode=pl.Buffered(1))
    return pl.pallas_call(
        _mix_body, grid=(T // tm,),
        in_specs=[pl.BlockSpec((tm, 512), row), pl.BlockSpec((tm, 512), row),
                  pl.BlockSpec((tm, D_MODEL), row), pl.BlockSpec((tm, D_MODEL), row),
                  pl.BlockSpec((tm, D_MODEL), row), full(w_oa), full(w_ob), full(w_out), full(ffn_norm),
                  full(w_router), full(b_router)],
        out_specs=[pl.BlockSpec((tm, D_MODEL), row), pl.BlockSpec((tm * ROW_SLABS, LANES), row),
                   pl.BlockSpec((tm, LANES), row), pl.BlockSpec((8, LANES), const)],
        out_shape=[jax.ShapeDtypeStruct((T, D_MODEL), F32), jax.ShapeDtypeStruct((T * ROW_SLABS, LANES), F32),
                   jax.ShapeDtypeStruct((T, LANES), F32), jax.ShapeDtypeStruct((8, LANES), F32)],
        scratch_shapes=[pltpu.VMEM((8, LANES), F32)],
        compiler_params=pltpu.CompilerParams(dimension_semantics=("arbitrary",), vmem_limit_bytes=VMEM_LIMIT),
        name="mix_route", interpret=interpret,
    )(o_a, o_b, g_a, g_b, x2d, w_oa, w_ob, w_out, ffn_norm, w_router, b_router)


class _RowRing:
    def __init__(self, step, n_steps, src_hbm, idx_ref, idx_next_ref, buf_ref, sems):
        self.step, self.n_steps, self.src = step, n_steps, src_hbm
        self.idx, self.idx_next, self.buf, self.sems = idx_ref, idx_next_ref, buf_ref, sems
        self.slot = step % 2
        self.n_rows = buf_ref.shape[1] // ROW_SLABS

    def _start(self, idx_ref, slot, lo, hi):
        for r in range(lo, hi):
            src = self.src.at[pl.ds(pl.multiple_of(idx_ref[0, 0, r] * ROW_SLABS, ROW_SLABS), ROW_SLABS), :]
            dst = self.buf.at[slot, pl.ds(r * ROW_SLABS, ROW_SLABS), :]
            pltpu.make_async_copy(src, dst, self.sems.at[slot]).start()

    def _wait(self, slot):
        pltpu.make_async_copy(self.buf.at[slot], self.buf.at[slot], self.sems.at[slot]).wait()

    def rows(self):
        @pl.when(self.step == 0)
        def _():
            self._start(self.idx, 0, 0, self.n_rows)

        self._wait(self.slot)
        return self.buf.at[self.slot]

    def start_next(self, part, n_parts):
        per = self.n_rows // n_parts
        self._start(self.idx_next, 1 - self.slot, part * per, (part + 1) * per)

    def drain(self):
        @pl.when(self.step == self.n_steps - 1)
        def _():
            self._wait(1 - self.slot)


def _moe_body(blk_e_ref, n_used_ref, tok_ref, tok_next_ref, h_hbm, wg_ref, wu_ref, wd_ref, y_ref, xs_ref, sems):
    del blk_e_ref
    step = pl.program_id(0)
    n_used = n_used_ref[0]

    @pl.when(step < n_used)
    def _():
        ring = _RowRing(step, n_used, h_hbm, tok_ref, tok_next_ref, xs_ref, sems)
        xs = _slab_rows(ring.rows(), 0, ring.n_rows).astype(BF16)
        ring.start_next(0, 4)
        gate = _dot(xs, wg_ref[0])
        ring.start_next(1, 4)
        up = _dot(xs, wu_ref[0])
        ring.start_next(2, 4)
        ys = _dot((jax.nn.silu(gate) * up).astype(BF16), wd_ref[0])
        ring.start_next(3, 4)
        _store_slab_rows(y_ref, ys)
        ring.drain()

    @pl.when(step >= n_used)
    def _():
        y_ref[...] = jnp.zeros_like(y_ref)


def _moe_call(blk_e, n_used, slot_tok, h2, w_g, w_u, w_d, interpret):
    n_blocks = slot_tok.shape[0]
    nb = EXPERT_BLOCK
    tok_spec = lambda off: pl.BlockSpec((1, 1, nb), lambda i, be, nu: (jnp.minimum(i + off, n_blocks - 1), 0, 0),
                                        memory_space=pltpu.SMEM)
    grid_spec = pltpu.PrefetchScalarGridSpec(
        num_scalar_prefetch=2, grid=(n_blocks,),
        in_specs=[tok_spec(0), tok_spec(1),
                  pl.BlockSpec(memory_space=pl.ANY),
                  pl.BlockSpec((1, D_MODEL, EXPERT_FF), lambda i, be, nu: (be[i], 0, 0)),
                  pl.BlockSpec((1, D_MODEL, EXPERT_FF), lambda i, be, nu: (be[i], 0, 0)),
                  pl.BlockSpec((1, EXPERT_FF, D_MODEL), lambda i, be, nu: (be[i], 0, 0))],
        out_specs=pl.BlockSpec((nb * ROW_SLABS, LANES), lambda i, be, nu: (i, 0)),
        scratch_shapes=[pltpu.VMEM((2, nb * ROW_SLABS, LANES), F32), pltpu.SemaphoreType.DMA((2,))],
    )
    return pl.pallas_call(
        _moe_body, grid_spec=grid_spec,
        out_shape=jax.ShapeDtypeStruct((n_blocks * nb * ROW_SLABS, LANES), F32),
        compiler_params=pltpu.CompilerParams(dimension_semantics=("arbitrary",), vmem_limit_bytes=VMEM_LIMIT),
        name="moe_experts", interpret=interpret,
    )(blk_e, n_used, slot_tok, slot_tok, h2, w_g, w_u, w_d)


def _final_body(dest_ref, dest_next_ref, x1_ref, route_ref, y_hbm, g_ref, o_ref, ybuf_ref, sems):
    tm = x1_ref.shape[0]
    ring = _RowRing(pl.program_id(0), pl.num_programs(0), y_hbm, dest_ref, dest_next_ref, ybuf_ref, sems)
    rows = ring.rows()
    n_parts = 4
    rc = tm // n_parts
    for c in range(n_parts):
        sl = pl.ds(c * rc, rc)
        route = route_ref[sl, :]
        lane = lax.broadcasted_iota(jnp.int32, route.shape, 1)
        w1 = jnp.sum(jnp.where(lane == ROUTE_W0, route, 0.0), axis=-1, keepdims=True)
        w2 = jnp.sum(jnp.where(lane == ROUTE_W0 + 1, route, 0.0), axis=-1, keepdims=True)
        y = x1_ref[sl, :] + w1 * _slab_rows(rows, c * rc, rc) + w2 * _slab_rows(rows, tm + c * rc, rc)
        o_ref[sl, :] = _rms(y, g_ref[...])
        ring.start_next(c, n_parts)
    ring.drain()


def _final_call(dest, x1, route, y, final_norm, tm, interpret):
    T = x1.shape[0]
    nt = T // tm
    dest_spec = lambda off: pl.BlockSpec((1, 1, EXPERT_TOPK * tm), lambda i: (jnp.minimum(i + off, nt - 1), 0, 0),
                                         memory_space=pltpu.SMEM)
    return pl.pallas_call(
        _final_body, grid=(nt,),
        in_specs=[dest_spec(0), dest_spec(1),
                  pl.BlockSpec((tm, D_MODEL), lambda i: (i, 0)),
                  pl.BlockSpec((tm, LANES), lambda i: (i, 0)),
                  pl.BlockSpec(memory_space=pl.ANY),
                  pl.BlockSpec((1, D_MODEL), lambda i: (0, 0))],
        out_specs=pl.BlockSpec((tm, D_MODEL), lambda i: (i, 0)),
        out_shape=jax.ShapeDtypeStruct((T, D_MODEL), F32),
        scratch_shapes=[pltpu.VMEM((2, EXPERT_TOPK * tm * ROW_SLABS, LANES), F32), pltpu.SemaphoreType.DMA((2,))],
        compiler_params=pltpu.CompilerParams(dimension_semantics=("arbitrary",), vmem_limit_bytes=VMEM_LIMIT),
        name="final_norm", interpret=interpret,
    )(dest, dest, x1, route, y, final_norm)


def _rope_tables(seq, rot_dim):
    half = rot_dim // 2
    inv_freq = jnp.power(ROPE_THETA, -jnp.arange(half, dtype=F32) / half)
    ang = jnp.arange(seq, dtype=F32)[:, None] * inv_freq[None, :]
    return jnp.cos(ang), jnp.sin(ang)


def _lane_tables(seq):
    cos, sin = _rope_tables(seq, MLA_ROPE)
    z = lambda n: jnp.zeros((seq, n), F32)
    o = lambda n: jnp.ones((seq, n), F32)
    mla = (jnp.concatenate([o(64), cos, cos, o(32)], 1),
           jnp.concatenate([z(64), -sin, z(48)], 1),
           jnp.concatenate([z(80), sin, z(32)], 1))
    cos, sin = _rope_tables(seq, MOBA_ROT)
    head = (jnp.concatenate([cos, cos, o(48)], 1), jnp.concatenate([-sin, z(56)], 1),
            jnp.concatenate([z(8), sin, z(48)], 1))
    moba = tuple(jnp.concatenate([t, t], 1) for t in head)
    return mla + moba


def _prep_weights(w_in, w_uq, w_ukv):
    kr = jnp.pad(w_in[:, 1024:1056], ((0, 0), (64, 32)))
    w_in_r = jnp.concatenate([w_in[:, :1024], kr, w_in[:, 1056:]], axis=1).astype(BF16)
    w_uq_r = jnp.pad(w_uq.reshape(Q_LORA, MLA_HEADS, MLA_NOPE + MLA_ROPE),
                     ((0, 0), (0, 0), (0, LANES - MLA_NOPE - MLA_ROPE))).reshape(Q_LORA, MLA_HEADS * LANES)
    w_kv = w_ukv.reshape(KV_LORA, MLA_HEADS, MLA_NOPE + MLA_V)
    w_uk_r = jnp.pad(w_kv[:, :, :MLA_NOPE], ((0, 0), (0, 0), (0, LANES - MLA_NOPE))).reshape(KV_LORA, -1)
    w_uv_r = w_kv[:, :, MLA_NOPE:].reshape(KV_LORA, MLA_HEADS * MLA_V)
    return w_in_r, w_uq_r.astype(BF16), w_uk_r.astype(BF16), w_uv_r.astype(BF16)


def _router_weights(w_rg, b_rg, w_re, b_re):
    w_e = jnp.transpose(w_re, (1, 0, 2)).reshape(D_MODEL, N_EXPERTS)
    w = jnp.concatenate([w_rg, w_e], axis=1)
    b = jnp.concatenate([b_rg, b_re.reshape(N_EXPERTS)])
    pad = LANES - N_GROUPS - N_EXPERTS
    return jnp.pad(w, ((0, 0), (0, pad))), jnp.pad(b, (0, pad))[None, :]


def _dispatch(route, counts, T, tm):
    eid = route[:, ROUTE_EID0:ROUTE_EID0 + EXPERT_TOPK].astype(jnp.int32)
    rank = route[:, ROUTE_RANK0:ROUTE_RANK0 + EXPERT_TOPK].astype(jnp.int32)
    cnt = counts[0, ROUTE_EXPERT0:ROUTE_EXPERT0 + N_EXPERTS].astype(jnp.int32)
    padded = (cnt + EXPERT_BLOCK - 1) // EXPERT_BLOCK * EXPERT_BLOCK
    pend = jnp.cumsum(padded)
    pstart = pend - padded
    dest = pstart[eid] + rank
    n_blocks = T * EXPERT_TOPK // EXPERT_BLOCK + N_EXPERTS
    n_slots = n_blocks * EXPERT_BLOCK
    tok = jnp.repeat(jnp.arange(T, dtype=jnp.int32), EXPERT_TOPK)
    slot_tok = jnp.zeros((n_slots,), jnp.int32).at[dest.reshape(-1)].set(tok)
    blk_start = jnp.arange(n_blocks, dtype=jnp.int32) * EXPERT_BLOCK
    blk_e = jnp.minimum(jnp.sum(pend[None, :] <= blk_start[:, None], axis=1), N_EXPERTS - 1).astype(jnp.int32)
    n_used = (pend[-1] // EXPERT_BLOCK).astype(jnp.int32)[None]
    dest_tiles = dest.reshape(T // tm, tm, EXPERT_TOPK).transpose(0, 2, 1).reshape(T // tm, 1, EXPERT_TOPK * tm)
    return blk_e, n_used, slot_tok.reshape(n_blocks, 1, EXPERT_BLOCK), dest_tiles


def _forward(x, attn_norm, w_in, q_norm, w_uq, kv_norm, w_ukv, w_o_mla, w_o_moba, w_out, ffn_norm,
             w_router_group, b_router_group, w_router_expert, b_router_expert, w_exp_gate, w_exp_up,
             w_exp_down, final_norm, *, tm=256, tm_dense=512, tq_mla=512, tk_mla=1024, moba_group=4,
             interpret=False):
    B, S, _ = x.shape
    T = B * S
    x2d = x.reshape(T, D_MODEL)
    tabs = _lane_tables(S)
    for l in range(attn_norm.shape[0]):
        w_in_r, w_uq_r, w_uk_r, w_uv_r = _prep_weights(w_in[l], w_uq[l], w_ukv[l])
        q_m, k_m, v_m, q_b, k_b, v_b, kmean, g_a, g_b = _proj_call(
            x2d, attn_norm[l][None], w_in_r, q_norm[l][None], w_uq_r, kv_norm[l][None], w_uk_r, w_uv_r,
            tabs, S, tm_dense, interpret)
        o_a = _mla_call(q_m, k_m, v_m, B, S, tq_mla, tk_mla, interpret)
        o_b = _moba_call(q_b, k_b, v_b, kmean.reshape(B, S // MOBA_BLOCK, 512), B, S, moba_group, interpret)
        w_r, b_r = _router_weights(w_router_group[l], b_router_group[l], w_router_expert[l], b_router_expert[l])
        x1, h2, route, counts = _mix_call(o_a, o_b, g_a, g_b, x2d, w_o_mla[l].astype(BF16),
                                          w_o_moba[l].astype(BF16), w_out[l].astype(BF16), ffn_norm[l][None],
                                          w_r, b_r, tm_dense, interpret)
        blk_e, n_used, slot_tok, dest_tiles = _dispatch(route, counts, T, tm)
        y = _moe_call(blk_e, n_used, slot_tok, h2, w_exp_gate[l].astype(BF16),
                      w_exp_up[l].astype(BF16), w_exp_down[l].astype(BF16), interpret)
        assert attn_norm.shape[0] == 1
    return _final_call(dest_tiles, x1, route, y, final_norm[None], tm, interpret).reshape(B, S, D_MODEL)


def kernel(x, attn_norm, w_in, q_norm, w_uq, kv_norm, w_ukv, w_o_mla, w_o_moba, w_out, ffn_norm,
           w_router_group, b_router_group, w_router_expert, b_router_expert, w_exp_gate, w_exp_up,
           w_exp_down, final_norm):
    return _forward(x, attn_norm, w_in, q_norm, w_uq, kv_norm, w_ukv, w_o_mla, w_o_moba, w_out, ffn_norm,
                    w_router_group, b_router_group, w_router_expert, b_router_expert, w_exp_gate, w_exp_up,
                    w_exp_down, final_norm)
```

```python
import functools

import jax
import jax.numpy as jnp
import numpy as np
from jax import lax
from jax.experimental import pallas as pl
from jax.experimental.pallas import tpu as pltpu

F32 = jnp.float32
BF16 = jnp.bfloat16

D_MODEL = 1024
ROPE_THETA = 500000.0
NORM_EPS = 1e-6
MLA_HEADS = 8
MLA_NOPE = 64
MLA_ROPE = 32
MLA_V = 64
KV_LORA = 256
Q_LORA = 768
MOBA_HEADS = 8
MOBA_HD = 64
MOBA_ROT = 16
MOBA_BLOCK = 256
MOBA_TOPK = 3
N_GROUPS = 4
EXPERTS_PER_GROUP = 8
N_EXPERTS = 32
EXPERT_TOPK = 2
EXPERT_FF = 256
EXPERT_BLOCK = 256

LANES = 128
ROW_SLABS = D_MODEL // LANES
HEAD_PAIRS = MLA_HEADS // 2
NEG_BIG = -1e30
LOG2_E = 1.4426950408889634
VMEM_LIMIT = 56 * 1024 * 1024

SEG_CQ = (0, 768)
SEG_CKV = (768, 1024)
SEG_KR = (1024, 1152)
SEG_QB = (1152, 1664)
SEG_KB = (1664, 2176)
SEG_VB = (2176, 2688)
SEG_GA = (2688, 3712)
SEG_GB = (3712, 4736)
IN_WIDTH_R = 4736


def _dot(a, b):
    return jnp.dot(a, b, preferred_element_type=F32)


def _dot_nt(a, b):
    return lax.dot_general(a, b, (((1,), (1,)), ((), ())), preferred_element_type=F32)


def _rms(x, g):
    return x * lax.rsqrt(jnp.mean(x * x, axis=-1, keepdims=True) + NORM_EPS) * g


def _rope_tile(x, c, s_lo, s_hi, half):
    return x * c + pltpu.roll(x, LANES - half, 1) * s_lo + pltpu.roll(x, half, 1) * s_hi


def _proj_body(x_ref, g_ref, win_ref, qn_ref, wuq_ref, kvn_ref, wuk_ref, wuv_ref,
               mc_ref, ms1_ref, ms2_ref, bc_ref, bs1_ref, bs2_ref,
               qm_ref, km_ref, vm_ref, qb_ref, kb_ref, vb_ref, kmean_ref, ga_ref, gb_ref):
    tm = x_ref.shape[0]
    h = _rms(x_ref[...], g_ref[...]).astype(BF16)

    def seg(s):
        return _dot(h, win_ref[:, s[0]:s[1]])

    mc, ms1, ms2 = mc_ref[...], ms1_ref[...], ms2_ref[...]
    bc, bs1, bs2 = bc_ref[...], bs1_ref[...], bs2_ref[...]

    hq = _rms(seg(SEG_CQ), qn_ref[...]).astype(BF16)
    q = _dot(hq, wuq_ref[...])
    q_scale = (MLA_NOPE + MLA_ROPE) ** -0.5 * LOG2_E
    for hd in range(MLA_HEADS):
        sl = slice(hd * LANES, (hd + 1) * LANES)
        qm_ref[:, sl] = (_rope_tile(q[:, sl], mc, ms1, ms2, MLA_ROPE // 2) * q_scale).astype(BF16)

    hkv = _rms(seg(SEG_CKV), kvn_ref[...]).astype(BF16)
    k_nope = _dot(hkv, wuk_ref[...])
    kr = _rope_tile(seg(SEG_KR), mc, ms1, ms2, MLA_ROPE // 2)
    for hd in range(MLA_HEADS):
        sl = slice(hd * LANES, (hd + 1) * LANES)
        km_ref[:, sl] = (k_nope[:, sl] + kr).astype(BF16)
    vm_ref[...] = _dot(hkv, wuv_ref[...]).astype(BF16)

    qb = seg(SEG_QB)
    kb = seg(SEG_KB)
    qb_scale = MOBA_HD ** -0.5 * LOG2_E
    n_blk = tm // MOBA_BLOCK
    for t in range(MOBA_HEADS * MOBA_HD // LANES):
        sl = slice(t * LANES, (t + 1) * LANES)
        qb_ref[:, sl] = (_rope_tile(qb[:, sl], bc, bs1, bs2, MOBA_ROT // 2) * qb_scale).astype(BF16)
        kt = _rope_tile(kb[:, sl], bc, bs1, bs2, MOBA_ROT // 2)
        kb_ref[:, sl] = kt.astype(BF16)
        for j in range(n_blk):
            kmean_ref[j, :, sl] = jnp.mean(kt[j * MOBA_BLOCK:(j + 1) * MOBA_BLOCK], axis=0, keepdims=True)
    vb_ref[...] = seg(SEG_VB).astype(BF16)
    ga_ref[...] = seg(SEG_GA).astype(BF16)
    gb_ref[...] = seg(SEG_GB).astype(BF16)


def _proj_call(x2d, g_attn, w_in_r, q_norm, w_uq_r, kv_norm, w_uk_r, w_uv_r, tabs, seq, tm, interpret):
    T = x2d.shape[0]
    n_seq_tiles = seq // tm
    row = lambda i: (i, 0)
    const = lambda i: (0, 0)
    tab = lambda i: (i % n_seq_tiles, 0)
    full = lambda a: pl.BlockSpec(a.shape, const, pipeline_mode=pl.Buffered(1))
    n_blk = tm // MOBA_BLOCK
    out_shapes = [
        jax.ShapeDtypeStruct((T, MLA_HEADS * LANES), BF16),
        jax.ShapeDtypeStruct((T, MLA_HEADS * LANES), BF16),
        jax.ShapeDtypeStruct((T, MLA_HEADS * MLA_V), BF16),
        jax.ShapeDtypeStruct((T, 512), BF16),
        jax.ShapeDtypeStruct((T, 512), BF16),
        jax.ShapeDtypeStruct((T, 512), BF16),
        jax.ShapeDtypeStruct((T // MOBA_BLOCK, 1, 512), F32),
        jax.ShapeDtypeStruct((T, D_MODEL), BF16),
        jax.ShapeDtypeStruct((T, D_MODEL), BF16),
    ]
    out_specs = [
        pl.BlockSpec((tm, 1024), row), pl.BlockSpec((tm, 1024), row), pl.BlockSpec((tm, 512), row),
        pl.BlockSpec((tm, 512), row), pl.BlockSpec((tm, 512), row), pl.BlockSpec((tm, 512), row),
        pl.BlockSpec((n_blk, 1, 512), lambda i: (i, 0, 0)),
        pl.BlockSpec((tm, 1024), row), pl.BlockSpec((tm, 1024), row),
    ]
    in_specs = [pl.BlockSpec((tm, D_MODEL), row), full(g_attn), full(w_in_r), full(q_norm), full(w_uq_r),
                full(kv_norm), full(w_uk_r), full(w_uv_r)] + [pl.BlockSpec((tm, LANES), tab)] * 6
    return pl.pallas_call(
        _proj_body, grid=(T // tm,), in_specs=in_specs, out_specs=out_specs, out_shape=out_shapes,
        compiler_params=pltpu.CompilerParams(dimension_semantics=("parallel",), vmem_limit_bytes=VMEM_LIMIT),
        name="proj", interpret=interpret,
    )(x2d, g_attn, w_in_r, q_norm, w_uq_r, kv_norm, w_uk_r, w_uv_r, *tabs)


def _softmax_step(s, m, l):
    m_new = jnp.maximum(m, jnp.max(s, axis=-1, keepdims=True))
    alpha = jnp.exp2(m - m_new)
    p = jnp.exp2(s - m_new)
    return m_new, alpha, alpha * l + jnp.sum(p, axis=-1, keepdims=True), p.astype(BF16)


def _pair_update(state, s0, s1, v, lo):
    m0, l0, m1, l1, acc = state
    m0, a0, l0, p0 = _softmax_step(s0, m0, l0)
    m1, a1, l1, p1 = _softmax_step(s1, m1, l1)
    acc = acc * jnp.where(lo, a0, a1) + jnp.where(lo, _dot(p0, v), _dot(p1, v))
    return m0, l0, m1, l1, acc


def _init_state(tq):
    col = lambda val: jnp.full((tq, 1), val, F32)
    return col(-jnp.inf), col(0.0), col(-jnp.inf), col(0.0), jnp.zeros((tq, LANES), F32)


def _flash_sweep(n_full, scores, values, n_pairs, q_pos0, tq, tk, lo):
    def update(states, s, v):
        return tuple(_pair_update(states[p], s[2 * p], s[2 * p + 1], v[p], lo) for p in range(n_pairs))

    states = lax.fori_loop(0, n_full, lambda j, st: update(st, scores(j), values(j)),
                           tuple(_init_state(tq) for _ in range(n_pairs)))
    kpos = n_full * tk + lax.broadcasted_iota(jnp.int32, (tq, tk), 1)
    causal = kpos <= q_pos0 + lax.broadcasted_iota(jnp.int32, (tq, tk), 0)
    states = update(states, [jnp.where(causal, s, -jnp.inf) for s in scores(n_full)], values(n_full))
    return [acc / jnp.where(lo, l0, l1) for (_, l0, _, l1, acc) in states]


def _lane_tile(ref, rows, t):
    return ref[rows, t * LANES:(t + 1) * LANES]


def _mla_body(q_ref, k_ref, v_ref, o_ref, *, tk, n_pairs):
    tq = q_ref.shape[0]
    qi = pl.program_id(2)
    lo = lax.broadcasted_iota(jnp.int32, (1, LANES), 1) < MLA_V
    q = [_lane_tile(q_ref, slice(None), h) for h in range(2 * n_pairs)]
    rows = lambda j: pl.ds(pl.multiple_of(j * tk, tk), tk)
    scores = lambda j: [_dot_nt(q[h], _lane_tile(k_ref, rows(j), h)) for h in range(2 * n_pairs)]
    values = lambda j: [_lane_tile(v_ref, rows(j), p) for p in range(n_pairs)]
    outs = _flash_sweep((qi * tq) // tk, scores, values, n_pairs, qi * tq, tq, tk, lo)
    for p in range(n_pairs):
        o_ref[:, p * LANES:(p + 1) * LANES] = outs[p].astype(o_ref.dtype)


def _mla_call(q_mla, k_mla, v_mla, batch, seq, tq, tk, n_pairs, interpret):
    T = q_mla.shape[0]
    nq = seq // tq
    wide, narrow = 2 * n_pairs * LANES, n_pairs * LANES
    return pl.pallas_call(
        functools.partial(_mla_body, tk=tk, n_pairs=n_pairs), grid=(batch, HEAD_PAIRS // n_pairs, nq),
        in_specs=[pl.BlockSpec((tq, wide), lambda b, g, i: (b * nq + i, g)),
                  pl.BlockSpec((seq, wide), lambda b, g, i: (b, g)),
                  pl.BlockSpec((seq, narrow), lambda b, g, i: (b, g))],
        out_specs=pl.BlockSpec((tq, narrow), lambda b, g, i: (b * nq + i, g)),
        out_shape=jax.ShapeDtypeStruct((T, MLA_HEADS * MLA_V), BF16),
        compiler_params=pltpu.CompilerParams(dimension_semantics=("parallel", "parallel", "arbitrary"),
                                             vmem_limit_bytes=VMEM_LIMIT),
        name="mla_attn", interpret=interpret,
    )(q_mla, k_mla, v_mla)


def _top_blocks_bias(gate_t, n_past):
    blk = lax.broadcasted_iota(jnp.int32, gate_t.shape, 0)
    g = jnp.where(blk < n_past, gate_t, -jnp.inf)
    sel = blk == n_past
    for _ in range(MOBA_TOPK):
        mx = jnp.max(g, axis=0, keepdims=True)
        cand = (g == mx) & (mx > -jnp.inf)
        first = jnp.min(jnp.where(cand, blk, LANES), axis=0, keepdims=True)
        pick = blk == first
        sel = sel | pick
        g = jnp.where(pick, -jnp.inf, g)
    return jnp.where(sel, 0.0, NEG_BIG)


def _moba_body(q_ref, k_ref, v_ref, kmean_ref, blk_ref, o_ref, *, group, n_pairs):
    tq = q_ref.shape[0]
    qi = pl.program_id(2)
    lo = lax.broadcasted_iota(jnp.int32, (1, LANES), 1) < MOBA_HD
    n_blocks = kmean_ref.shape[1]

    def aug(qh, kmean):
        gate_t = lax.dot_general(kmean, qh.astype(F32), (((1,), (1,)), ((), ())),
                                 precision=lax.Precision.HIGHEST, preferred_element_type=F32)
        bias_t = jnp.concatenate([_top_blocks_bias(gate_t, qi), jnp.zeros((LANES - n_blocks, tq), F32)], axis=0)
        return jnp.concatenate([qh, bias_t.T.astype(BF16)], axis=1)

    qa = []
    for p in range(n_pairs):
        q = _lane_tile(q_ref, slice(None), p)
        kmean = kmean_ref[0, :, p * LANES:(p + 1) * LANES]
        zq = jnp.zeros_like(q)
        qa += [aug(jnp.where(lo, q, zq), kmean), aug(jnp.where(lo, zq, q), kmean)]
    tk = group * tq
    rows = lambda j: pl.ds(pl.multiple_of(j * tk, tk), tk)

    def scores(j):
        out = []
        for p in range(n_pairs):
            k_aug = jnp.concatenate([_lane_tile(k_ref, rows(j), p), blk_ref[rows(j), :]], axis=1)
            out += [_dot_nt(qa[2 * p], k_aug), _dot_nt(qa[2 * p + 1], k_aug)]
        return out

    values = lambda j: [_lane_tile(v_ref, rows(j), p) for p in range(n_pairs)]
    outs = _flash_sweep(qi // group, scores, values, n_pairs, qi * tq, tq, tk, lo)
    for p in range(n_pairs):
        o_ref[:, p * LANES:(p + 1) * LANES] = outs[p].astype(o_ref.dtype)


def _moba_call(q_b, k_b, v_b, kmean, batch, seq, group, n_pairs, interpret):
    T = q_b.shape[0]
    tq = MOBA_BLOCK
    nq = seq // tq
    width = n_pairs * LANES
    tile = pl.BlockSpec((tq, width), lambda b, g, i: (b * nq + i, g))
    kv = pl.BlockSpec((seq, width), lambda b, g, i: (b, g))
    blk_onehot = (jnp.arange(seq, dtype=jnp.int32)[:, None] // MOBA_BLOCK
                  == jnp.arange(LANES, dtype=jnp.int32)[None, :]).astype(BF16)
    return pl.pallas_call(
        functools.partial(_moba_body, group=group, n_pairs=n_pairs), grid=(batch, HEAD_PAIRS // n_pairs, nq),
        in_specs=[tile, kv, kv,
                  pl.BlockSpec((1, nq, width), lambda b, g, i: (b, 0, g)),
                  pl.BlockSpec((seq, LANES), lambda b, g, i: (0, 0))],
        out_specs=tile,
        out_shape=jax.ShapeDtypeStruct((T, MOBA_HEADS * MOBA_HD), BF16),
        compiler_params=pltpu.CompilerParams(dimension_semantics=("parallel", "parallel", "arbitrary"),
                                             vmem_limit_bytes=VMEM_LIMIT),
        name="moba_attn", interpret=interpret,
    )(q_b, k_b, v_b, kmean, blk_onehot)


def _slab_rows(ref, row0, n):
    return jnp.concatenate([ref[pl.ds(row0 * ROW_SLABS + s, n, stride=ROW_SLABS), :] for s in range(ROW_SLABS)],
                           axis=1)


def _store_slab_rows(ref, val):
    for s in range(ROW_SLABS):
        ref[pl.ds(s, val.shape[0], stride=ROW_SLABS), :] = val[:, s * LANES:(s + 1) * LANES]


ROUTE_GROUP0 = 0
ROUTE_EXPERT0 = N_GROUPS
ROUTE_EID0, ROUTE_W0, ROUTE_RANK0 = 0, 2, 4


def _first_lane_of_max(vals, lane):
    mx = jnp.max(vals, axis=-1, keepdims=True)
    return mx, jnp.min(jnp.where(vals == mx, lane, LANES), axis=-1, keepdims=True)


def _mix_body(oa_ref, ob_ref, ga_ref, gb_ref, x_ref, woa_ref, wob_ref, wout_ref, fn_ref, wr_ref, br_ref,
              x1_ref, h2_ref, route_ref, counts_ref, carry_ref):
    tm = x_ref.shape[0]

    @pl.when(pl.program_id(0) == 0)
    def _():
        carry_ref[...] = jnp.zeros_like(carry_ref)

    a = _dot(oa_ref[...], woa_ref[...])
    b = _dot(ob_ref[...], wob_ref[...])
    mixed = jax.nn.sigmoid(ga_ref[...].astype(F32)) * a + jax.nn.sigmoid(gb_ref[...].astype(F32)) * b
    x1 = x_ref[...] + _dot(mixed.astype(BF16), wout_ref[...])
    x1_ref[...] = x1
    h2 = _rms(x1, fn_ref[...])
    _store_slab_rows(h2_ref, h2)

    logits = jnp.dot(h2, wr_ref[...], precision=lax.Precision.HIGHEST, preferred_element_type=F32) + br_ref[...]
    lane = lax.broadcasted_iota(jnp.int32, (tm, LANES), 1)
    neg = -jnp.inf
    g_log = jnp.where(lane < N_GROUPS, logits, neg)
    g_max, g_sel = _first_lane_of_max(g_log, lane)
    p_group = 1.0 / jnp.sum(jnp.exp(g_log - g_max), axis=-1, keepdims=True)
    e_lo = ROUTE_EXPERT0 + g_sel * EXPERTS_PER_GROUP
    e_log = jnp.where((lane >= e_lo) & (lane < e_lo + EXPERTS_PER_GROUP), logits, neg)
    v1, i1 = _first_lane_of_max(e_log, lane)
    v2, i2 = _first_lane_of_max(jnp.where(lane == i1, neg, e_log), lane)
    e2 = jnp.exp(v2 - v1)
    w1 = p_group * (1.0 / (1.0 + e2))
    w2 = p_group * (e2 / (1.0 + e2))

    pick1, pick2 = lane == i1, lane == i2
    onehot = (pick1 | pick2).astype(F32)
    tri = (lax.broadcasted_iota(jnp.int32, (tm, tm), 1) < lax.broadcasted_iota(jnp.int32, (tm, tm), 0))
    before = _dot(tri.astype(BF16), onehot.astype(BF16)) + carry_ref[0:1, :]
    r1 = jnp.sum(jnp.where(pick1, before, 0.0), axis=-1, keepdims=True)
    r2 = jnp.sum(jnp.where(pick2, before, 0.0), axis=-1, keepdims=True)
    new_carry = carry_ref[0:1, :] + jnp.sum(onehot, axis=0, keepdims=True)
    carry_ref[...] = jnp.broadcast_to(new_carry, carry_ref.shape)
    counts_ref[...] = jnp.broadcast_to(new_carry, counts_ref.shape)

    fi1 = (i1 - ROUTE_EXPERT0).astype(F32)
    fi2 = (i2 - ROUTE_EXPERT0).astype(F32)
    route = jnp.zeros((tm, LANES), F32)
    for k, val in ((ROUTE_EID0, fi1), (ROUTE_EID0 + 1, fi2), (ROUTE_W0, w1), (ROUTE_W0 + 1, w2),
                   (ROUTE_RANK0, r1), (ROUTE_RANK0 + 1, r2)):
        route = jnp.where(lane == k, val, route)
    route_ref[...] = route


def _mix_call(o_a, o_b, g_a, g_b, x2d, w_oa, w_ob, w_out, ffn_norm, w_router, b_router, tm, interpret):
    T = x2d.shape[0]
    row = lambda i: (i, 0)
    const = lambda i: (0, 0)
    full = lambda a: pl.BlockSpec(a.shape, const, pipeline_mode=pl.Buffered(1))
    return pl.pallas_call(
        _mix_body, grid=(T // tm,),
        in_specs=[pl.BlockSpec((tm, 512), row), pl.BlockSpec((tm, 512), row),
                  pl.BlockSpec((tm, D_MODEL), row), pl.BlockSpec((tm, D_MODEL), row),
                  pl.BlockSpec((tm, D_MODEL), row), full(w_oa), full(w_ob), full(w_out), full(ffn_norm),
                  full(w_router), full(b_router)],
        out_specs=[pl.BlockSpec((tm, D_MODEL), row), pl.BlockSpec((tm * ROW_SLABS, LANES), row),
                   pl.BlockSpec((tm, LANES), row), pl.BlockSpec((8, LANES), const)],
        out_shape=[jax.ShapeDtypeStruct((T, D_MODEL), F32), jax.ShapeDtypeStruct((T * ROW_SLABS, LANES), F32),
                   jax.ShapeDtypeStruct((T, LANES), F32), jax.ShapeDtypeStruct((8, LANES), F32)],
        scratch_shapes=[pltpu.VMEM((8, LANES), F32)],
        compiler_params=pltpu.CompilerParams(dimension_semantics=("arbitrary",), vmem_limit_bytes=VMEM_LIMIT),
        name="mix_route", interpret=interpret,
    )(o_a, o_b, g_a, g_b, x2d, w_oa, w_ob, w_out, ffn_norm, w_router, b_router)


class _RowRing:
    def __init__(self, step, n_steps, src_hbm, idx_ref, idx_next_ref, buf_ref, sems):
        self.step, self.n_steps, self.src = step, n_steps, src_hbm
        self.idx, self.idx_next, self.buf, self.sems = idx_ref, idx_next_ref, buf_ref, sems
        self.slot = step % 2
        self.n_rows = buf_ref.shape[1] // ROW_SLABS

    def _start(self, idx_ref, slot, lo, hi):
        for r in range(lo, hi):
            src = self.src.at[pl.ds(pl.multiple_of(idx_ref[0, 0, r] * ROW_SLABS, ROW_SLABS), ROW_SLABS), :]
            dst = self.buf.at[slot, pl.ds(r * ROW_SLABS, ROW_SLABS), :]
            pltpu.make_async_copy(src, dst, self.sems.at[slot]).start(priority=r % 2)

    def _wait(self, slot):
        pltpu.make_async_copy(self.buf.at[slot], self.buf.at[slot], self.sems.at[slot]).wait()

    def rows(self):
        @pl.when(self.step == 0)
        def _():
            self._start(self.idx, 0, 0, self.n_rows)

        self._wait(self.slot)
        return self.buf.at[self.slot]

    def start_next(self, part, n_parts):
        per = self.n_rows // n_parts
        self._start(self.idx_next, 1 - self.slot, part * per, (part + 1) * per)

    def drain(self):
        @pl.when(self.step == self.n_steps - 1)
        def _():
            self._wait(1 - self.slot)


def _moe_body(blk_e_ref, n_used_ref, tok_ref, tok_next_ref, h_hbm, wg_ref, wu_ref, wd_ref, y_ref, xs_ref, sems):
    del blk_e_ref
    step = pl.program_id(0)
    n_used = n_used_ref[0]

    @pl.when(step < n_used)
    def _():
        ring = _RowRing(step, n_used, h_hbm, tok_ref, tok_next_ref, xs_ref, sems)
        xs = _slab_rows(ring.rows(), 0, ring.n_rows).astype(BF16)
        ring.start_next(0, 4)
        gate = _dot(xs, wg_ref[0])
        ring.start_next(1, 4)
        up = _dot(xs, wu_ref[0])
        ring.start_next(2, 4)
        ys = _dot((jax.nn.silu(gate) * up).astype(BF16), wd_ref[0])
        ring.start_next(3, 4)
        _store_slab_rows(y_ref, ys)
        ring.drain()

    @pl.when(step >= n_used)
    def _():
        y_ref[...] = jnp.zeros_like(y_ref)


def _moe_call(blk_e, n_used, slot_tok, h2, w_g, w_u, w_d, interpret):
    n_blocks = slot_tok.shape[0]
    nb = EXPERT_BLOCK
    tok_spec = lambda off: pl.BlockSpec((1, 1, nb), lambda i, be, nu: (jnp.minimum(i + off, n_blocks - 1), 0, 0),
                                        memory_space=pltpu.SMEM)
    grid_spec = pltpu.PrefetchScalarGridSpec(
        num_scalar_prefetch=2, grid=(n_blocks,),
        in_specs=[tok_spec(0), tok_spec(1),
                  pl.BlockSpec(memory_space=pl.ANY),
                  pl.BlockSpec((1, D_MODEL, EXPERT_FF), lambda i, be, nu: (be[i], 0, 0)),
                  pl.BlockSpec((1, D_MODEL, EXPERT_FF), lambda i, be, nu: (be[i], 0, 0)),
                  pl.BlockSpec((1, EXPERT_FF, D_MODEL), lambda i, be, nu: (be[i], 0, 0))],
        out_specs=pl.BlockSpec((nb * ROW_SLABS, LANES), lambda i, be, nu: (i, 0)),
        scratch_shapes=[pltpu.VMEM((2, nb * ROW_SLABS, LANES), F32), pltpu.SemaphoreType.DMA((2,))],
    )
    return pl.pallas_call(
        _moe_body, grid_spec=grid_spec,
        out_shape=jax.ShapeDtypeStruct((n_blocks * nb * ROW_SLABS, LANES), F32),
        compiler_params=pltpu.CompilerParams(dimension_semantics=("arbitrary",), vmem_limit_bytes=VMEM_LIMIT),
        name="moe_experts", interpret=interpret,
    )(blk_e, n_used, slot_tok, slot_tok, h2, w_g, w_u, w_d)


def _final_body(dest_ref, dest_next_ref, x1_ref, route_ref, y_hbm, g_ref, o_ref, ybuf_ref, sems):
    tm = x1_ref.shape[0]
    ring = _RowRing(pl.program_id(0), pl.num_programs(0), y_hbm, dest_ref, dest_next_ref, ybuf_ref, sems)
    rows = ring.rows()
    n_parts = 4
    rc = tm // n_parts
    for c in range(n_parts):
        sl = pl.ds(c * rc, rc)
        route = route_ref[sl, :]
        lane = lax.broadcasted_iota(jnp.int32, route.shape, 1)
        w1 = jnp.sum(jnp.where(lane == ROUTE_W0, route, 0.0), axis=-1, keepdims=True)
        w2 = jnp.sum(jnp.where(lane == ROUTE_W0 + 1, route, 0.0), axis=-1, keepdims=True)
        y = x1_ref[sl, :] + w1 * _slab_rows(rows, c * rc, rc) + w2 * _slab_rows(rows, tm + c * rc, rc)
        o_ref[sl, :] = _rms(y, g_ref[...])
        ring.start_next(c, n_parts)
    ring.drain()


def _final_call(dest, x1, route, y, final_norm, tm, interpret):
    T = x1.shape[0]
    nt = T // tm
    dest_spec = lambda off: pl.BlockSpec((1, 1, EXPERT_TOPK * tm), lambda i: (jnp.minimum(i + off, nt - 1), 0, 0),
                                         memory_space=pltpu.SMEM)
    return pl.pallas_call(
        _final_body, grid=(nt,),
        in_specs=[dest_spec(0), dest_spec(1),
                  pl.BlockSpec((tm, D_MODEL), lambda i: (i, 0)),
                  pl.BlockSpec((tm, LANES), lambda i: (i, 0)),
                  pl.BlockSpec(memory_space=pl.ANY),
                  pl.BlockSpec((1, D_MODEL), lambda i: (0, 0))],
        out_specs=pl.BlockSpec((tm, D_MODEL), lambda i: (i, 0)),
        out_shape=jax.ShapeDtypeStruct((T, D_MODEL), F32),
        scratch_shapes=[pltpu.VMEM((2, EXPERT_TOPK * tm * ROW_SLABS, LANES), F32), pltpu.SemaphoreType.DMA((2,))],
        compiler_params=pltpu.CompilerParams(dimension_semantics=("arbitrary",), vmem_limit_bytes=VMEM_LIMIT),
        name="final_norm", interpret=interpret,
    )(dest, dest, x1, route, y, final_norm)


def _rope_tables(seq, rot_dim):
    half = rot_dim // 2
    inv_freq = jnp.power(ROPE_THETA, -jnp.arange(half, dtype=F32) / half)
    ang = jnp.arange(seq, dtype=F32)[:, None] * inv_freq[None, :]
    return jnp.cos(ang), jnp.sin(ang)


def _lane_tables(seq):
    cos, sin = _rope_tables(seq, MLA_ROPE)
    z = lambda n: jnp.zeros((seq, n), F32)
    o = lambda n: jnp.ones((seq, n), F32)
    mla = (jnp.concatenate([o(64), cos, cos, o(32)], 1),
           jnp.concatenate([z(64), -sin, z(48)], 1),
           jnp.concatenate([z(80), sin, z(32)], 1))
    cos, sin = _rope_tables(seq, MOBA_ROT)
    head = (jnp.concatenate([cos, cos, o(48)], 1), jnp.concatenate([-sin, z(56)], 1),
            jnp.concatenate([z(8), sin, z(48)], 1))
    moba = tuple(jnp.concatenate([t, t], 1) for t in head)
    return mla + moba


def _prep_weights(w_in, w_uq, w_ukv):
    kr = jnp.pad(w_in[:, 1024:1056], ((0, 0), (64, 32)))
    w_in_r = jnp.concatenate([w_in[:, :1024], kr, w_in[:, 1056:]], axis=1).astype(BF16)
    w_uq_r = jnp.pad(w_uq.reshape(Q_LORA, MLA_HEADS, MLA_NOPE + MLA_ROPE),
                     ((0, 0), (0, 0), (0, LANES - MLA_NOPE - MLA_ROPE))).reshape(Q_LORA, MLA_HEADS * LANES)
    w_kv = w_ukv.reshape(KV_LORA, MLA_HEADS, MLA_NOPE + MLA_V)
    w_uk_r = jnp.pad(w_kv[:, :, :MLA_NOPE], ((0, 0), (0, 0), (0, LANES - MLA_NOPE))).reshape(KV_LORA, -1)
    w_uv_r = w_kv[:, :, MLA_NOPE:].reshape(KV_LORA, MLA_HEADS * MLA_V)
    return w_in_r, w_uq_r.astype(BF16), w_uk_r.astype(BF16), w_uv_r.astype(BF16)


def _router_weights(w_rg, b_rg, w_re, b_re):
    w_e = jnp.transpose(w_re, (1, 0, 2)).reshape(D_MODEL, N_EXPERTS)
    w = jnp.concatenate([w_rg, w_e], axis=1)
    b = jnp.concatenate([b_rg, b_re.reshape(N_EXPERTS)])
    pad = LANES - N_GROUPS - N_EXPERTS
    return jnp.pad(w, ((0, 0), (0, pad))), jnp.pad(b, (0, pad))[None, :]


def _dispatch(route, counts, T, tm):
    eid = route[:, ROUTE_EID0:ROUTE_EID0 + EXPERT_TOPK].astype(jnp.int32)
    rank = route[:, ROUTE_RANK0:ROUTE_RANK0 + EXPERT_TOPK].astype(jnp.int32)
    cnt = counts[0, ROUTE_EXPERT0:ROUTE_EXPERT0 + N_EXPERTS].astype(jnp.int32)
    padded = (cnt + EXPERT_BLOCK - 1) // EXPERT_BLOCK * EXPERT_BLOCK
    pend = jnp.cumsum(padded)
    pstart = pend - padded
    dest = pstart[eid] + rank
    n_blocks = T * EXPERT_TOPK // EXPERT_BLOCK + N_EXPERTS
    n_slots = n_blocks * EXPERT_BLOCK
    tok = jnp.repeat(jnp.arange(T, dtype=jnp.int32), EXPERT_TOPK)
    slot_tok = jnp.zeros((n_slots,), jnp.int32).at[dest.reshape(-1)].set(
        tok, unique_indices=True, mode='promise_in_bounds')
    blk_start = jnp.arange(n_blocks, dtype=jnp.int32) * EXPERT_BLOCK
    blk_e = jnp.minimum(jnp.sum(pend[None, :] <= blk_start[:, None], axis=1), N_EXPERTS - 1).astype(jnp.int32)
    n_used = (pend[-1] // EXPERT_BLOCK).astype(jnp.int32)[None]
    dest_tiles = dest.reshape(T // tm, tm, EXPERT_TOPK).transpose(0, 2, 1).reshape(T // tm, 1, EXPERT_TOPK * tm)
    return blk_e, n_used, slot_tok.reshape(n_blocks, 1, EXPERT_BLOCK), dest_tiles


def _forward(x, attn_norm, w_in, q_norm, w_uq, kv_norm, w_ukv, w_o_mla, w_o_moba, w_out, ffn_norm,
             w_router_group, b_router_group, w_router_expert, b_router_expert, w_exp_gate, w_exp_up,
             w_exp_down, final_norm, *, tm=256, tm_dense=512, tq_mla=512, tk_mla=1024, moba_group=4,
             pairs_per_step=2, interpret=False):
    B, S, _ = x.shape
    T = B * S
    x2d = x.reshape(T, D_MODEL)
    tabs = _lane_tables(S)
    for l in range(attn_norm.shape[0]):
        w_in_r, w_uq_r, w_uk_r, w_uv_r = _prep_weights(w_in[l], w_uq[l], w_ukv[l])
        q_m, k_m, v_m, q_b, k_b, v_b, kmean, g_a, g_b = _proj_call(
            x2d, attn_norm[l][None], w_in_r, q_norm[l][None], w_uq_r, kv_norm[l][None], w_uk_r, w_uv_r,
            tabs, S, tm_dense, interpret)
        o_a = _mla_call(q_m, k_m, v_m, B, S, tq_mla, tk_mla, pairs_per_step, interpret)
        o_b = _moba_call(q_b, k_b, v_b, kmean.reshape(B, S // MOBA_BLOCK, 512), B, S, moba_group, pairs_per_step, interpret)
        w_r, b_r = _router_weights(w_router_group[l], b_router_group[l], w_router_expert[l], b_router_expert[l])
        x1, h2, route, counts = _mix_call(o_a, o_b, g_a, g_b, x2d, w_o_mla[l].astype(BF16),
                                          w_o_moba[l].astype(BF16), w_out[l].astype(BF16), ffn_norm[l][None],
                                          w_r, b_r, tm_dense, interpret)
        blk_e, n_used, slot_tok, dest_tiles = _dispatch(route, counts, T, tm)
        y = _moe_call(blk_e, n_used, slot_tok, h2, w_exp_gate[l].astype(BF16),
                      w_exp_up[l].astype(BF16), w_exp_down[l].astype(BF16), interpret)
        assert attn_norm.shape[0] == 1
    return _final_call(dest_tiles, x1, route, y, final_norm[None], tm, interpret).reshape(B, S, D_MODEL)


def kernel(x, attn_norm, w_in, q_norm, w_uq, kv_norm, w_ukv, w_o_mla, w_o_moba, w_out, ffn_norm,
           w_router_group, b_router_group, w_router_expert, b_router_expert, w_exp_gate, w_exp_up,
           w_exp_down, final_norm):
    return _forward(x, attn_norm, w_in, q_norm, w_uq, kv_norm, w_ukv, w_o_mla, w_o_moba, w_out, ffn_norm,
                    w_router_group, b_router_group, w_router_expert, b_router_expert, w_exp_gate, w_exp_up,
                    w_exp_down, final_norm)
```

```python
import functools

import jax
import jax.numpy as jnp
import numpy as np
from jax import lax
from jax.experimental import pallas as pl
from jax.experimental.pallas import tpu as pltpu

F32 = jnp.float32
BF16 = jnp.bfloat16

D_MODEL = 1024
ROPE_THETA = 500000.0
NORM_EPS = 1e-6
MLA_HEADS = 8
MLA_NOPE = 64
MLA_ROPE = 32
MLA_V = 64
KV_LORA = 256
Q_LORA = 768
MOBA_HEADS = 8
MOBA_HD = 64
MOBA_ROT = 16
MOBA_BLOCK = 256
MOBA_TOPK = 3
N_GROUPS = 4
EXPERTS_PER_GROUP = 8
N_EXPERTS = 32
EXPERT_TOPK = 2
EXPERT_FF = 256
EXPERT_BLOCK = 256

LANES = 128
ROW_SLABS = D_MODEL // LANES
HEAD_PAIRS = MLA_HEADS // 2
NEG_BIG = -1e30
LOG2_E = 1.4426950408889634
VMEM_LIMIT = 56 * 1024 * 1024

SEG_CQ = (0, 768)
SEG_CKV = (768, 1024)
SEG_KR = (1024, 1152)
SEG_QB = (1152, 1664)
SEG_KB = (1664, 2176)
SEG_VB = (2176, 2688)
SEG_GA = (2688, 3712)
SEG_GB = (3712, 4736)
IN_WIDTH_R = 4736


def _dot(a, b):
    return jnp.dot(a, b, preferred_element_type=F32)


def _dot_nt(a, b):
    return lax.dot_general(a, b, (((1,), (1,)), ((), ())), preferred_element_type=F32)


def _rms(x, g):
    return x * lax.rsqrt(jnp.mean(x * x, axis=-1, keepdims=True) + NORM_EPS) * g


def _rope_tile(x, c, s_lo, s_hi, half):
    return x * c + pltpu.roll(x, LANES - half, 1) * s_lo + pltpu.roll(x, half, 1) * s_hi


def _proj_body(x_ref, g_ref, win_ref, qn_ref, wuq_ref, kvn_ref, wuk_ref, wuv_ref,
               mc_ref, ms1_ref, ms2_ref, bc_ref, bs1_ref, bs2_ref,
               qm_ref, km_ref, vm_ref, qb_ref, kb_ref, vb_ref, kmean_ref, ga_ref, gb_ref):
    tm = x_ref.shape[0]
    h = _rms(x_ref[...], g_ref[...]).astype(BF16)

    def seg(s):
        return _dot(h, win_ref[:, s[0]:s[1]])

    mc, ms1, ms2 = mc_ref[...], ms1_ref[...], ms2_ref[...]
    bc, bs1, bs2 = bc_ref[...], bs1_ref[...], bs2_ref[...]

    hq = _rms(seg(SEG_CQ), qn_ref[...]).astype(BF16)
    q = _dot(hq, wuq_ref[...])
    q_scale = (MLA_NOPE + MLA_ROPE) ** -0.5 * LOG2_E
    for hd in range(MLA_HEADS):
        sl = slice(hd * LANES, (hd + 1) * LANES)
        qm_ref[:, sl] = (_rope_tile(q[:, sl], mc, ms1, ms2, MLA_ROPE // 2) * q_scale).astype(BF16)

    hkv = _rms(seg(SEG_CKV), kvn_ref[...]).astype(BF16)
    k_nope = _dot(hkv, wuk_ref[...])
    kr = _rope_tile(seg(SEG_KR), mc, ms1, ms2, MLA_ROPE // 2)
    for hd in range(MLA_HEADS):
        sl = slice(hd * LANES, (hd + 1) * LANES)
        km_ref[:, sl] = (k_nope[:, sl] + kr).astype(BF16)
    vm_ref[...] = _dot(hkv, wuv_ref[...]).astype(BF16)

    qb = seg(SEG_QB)
    kb = seg(SEG_KB)
    qb_scale = MOBA_HD ** -0.5 * LOG2_E
    n_blk = tm // MOBA_BLOCK
    for t in range(MOBA_HEADS * MOBA_HD // LANES):
        sl = slice(t * LANES, (t + 1) * LANES)
        qb_ref[:, sl] = (_rope_tile(qb[:, sl], bc, bs1, bs2, MOBA_ROT // 2) * qb_scale).astype(BF16)
        kt = _rope_tile(kb[:, sl], bc, bs1, bs2, MOBA_ROT // 2)
        kb_ref[0, sl, :] = kt.T.astype(BF16)
        for j in range(n_blk):
            kmean_ref[j, :, sl] = jnp.mean(kt[j * MOBA_BLOCK:(j + 1) * MOBA_BLOCK], axis=0, keepdims=True)
    vb_ref[...] = seg(SEG_VB).astype(BF16)
    ga_ref[...] = seg(SEG_GA).astype(BF16)
    gb_ref[...] = seg(SEG_GB).astype(BF16)


def _proj_call(x2d, g_attn, w_in_r, q_norm, w_uq_r, kv_norm, w_uk_r, w_uv_r, tabs, seq, tm, moba_tk, interpret):
    T = x2d.shape[0]
    n_seq_tiles = seq // tm
    tiles_per_chunk = moba_tk // tm
    row = lambda i: (i, 0)
    const = lambda i: (0, 0)
    tab = lambda i: (i % n_seq_tiles, 0)
    full = lambda a: pl.BlockSpec(a.shape, const, pipeline_mode=pl.Buffered(1))
    n_blk = tm // MOBA_BLOCK
    out_shapes = [
        jax.ShapeDtypeStruct((T, MLA_HEADS * LANES), BF16),
        jax.ShapeDtypeStruct((T, MLA_HEADS * LANES), BF16),
        jax.ShapeDtypeStruct((T, MLA_HEADS * MLA_V), BF16),
        jax.ShapeDtypeStruct((T, 512), BF16),
        jax.ShapeDtypeStruct((T // moba_tk, 512, moba_tk), BF16),
        jax.ShapeDtypeStruct((T, 512), BF16),
        jax.ShapeDtypeStruct((T // MOBA_BLOCK, 1, 512), F32),
        jax.ShapeDtypeStruct((T, D_MODEL), BF16),
        jax.ShapeDtypeStruct((T, D_MODEL), BF16),
    ]
    out_specs = [
        pl.BlockSpec((tm, 1024), row), pl.BlockSpec((tm, 1024), row), pl.BlockSpec((tm, 512), row),
        pl.BlockSpec((tm, 512), row),
        pl.BlockSpec((1, 512, tm), lambda i: (i // tiles_per_chunk, 0, i % tiles_per_chunk)),
        pl.BlockSpec((tm, 512), row),
        pl.BlockSpec((n_blk, 1, 512), lambda i: (i, 0, 0)),
        pl.BlockSpec((tm, 1024), row), pl.BlockSpec((tm, 1024), row),
    ]
    in_specs = [pl.BlockSpec((tm, D_MODEL), row), full(g_attn), full(w_in_r), full(q_norm), full(w_uq_r),
                full(kv_norm), full(w_uk_r), full(w_uv_r)] + [pl.BlockSpec((tm, LANES), tab)] * 6
    return pl.pallas_call(
        _proj_body, grid=(T // tm,), in_specs=in_specs, out_specs=out_specs, out_shape=out_shapes,
        compiler_params=pltpu.CompilerParams(dimension_semantics=("parallel",), vmem_limit_bytes=VMEM_LIMIT),
        name="proj", interpret=interpret,
    )(x2d, g_attn, w_in_r, q_norm, w_uq_r, kv_norm, w_uk_r, w_uv_r, *tabs)


def _softmax_step(s, m, l):
    m_new = jnp.maximum(m, jnp.max(s, axis=-1, keepdims=True))
    alpha = jnp.exp2(m - m_new)
    p = jnp.exp2(s - m_new)
    return m_new, alpha, alpha * l + jnp.sum(p, axis=-1, keepdims=True), p.astype(BF16)


def _pair_update(state, s0, s1, v, lo):
    m0, l0, m1, l1, acc = state
    m0, a0, l0, p0 = _softmax_step(s0, m0, l0)
    m1, a1, l1, p1 = _softmax_step(s1, m1, l1)
    acc = acc * jnp.where(lo, a0, a1) + jnp.where(lo, _dot(p0, v), _dot(p1, v))
    return m0, l0, m1, l1, acc


def _init_state(tq):
    col = lambda val: jnp.full((tq, 1), val, F32)
    return col(-jnp.inf), col(0.0), col(-jnp.inf), col(0.0), jnp.zeros((tq, LANES), F32)


def _flash_sweep(n_full, scores, values, n_pairs, q_pos0, tq, tk, lo):
    def update(states, s, v):
        return tuple(_pair_update(states[p], s[2 * p], s[2 * p + 1], v[p], lo) for p in range(n_pairs))

    states = lax.fori_loop(0, n_full, lambda j, st: update(st, scores(j), values(j)),
                           tuple(_init_state(tq) for _ in range(n_pairs)))
    kpos = n_full * tk + lax.broadcasted_iota(jnp.int32, (tq, tk), 1)
    causal = kpos <= q_pos0 + lax.broadcasted_iota(jnp.int32, (tq, tk), 0)
    states = update(states, [jnp.where(causal, s, -jnp.inf) for s in scores(n_full)], values(n_full))
    return [acc / jnp.where(lo, l0, l1) for (_, l0, _, l1, acc) in states]


def _lane_tile(ref, rows, t):
    return ref[rows, t * LANES:(t + 1) * LANES]


def _mla_body(q_ref, k_ref, v_ref, o_ref, *, tk, n_pairs):
    tq = q_ref.shape[0]
    qi = pl.program_id(2)
    lo = lax.broadcasted_iota(jnp.int32, (1, LANES), 1) < MLA_V
    q = [_lane_tile(q_ref, slice(None), h) for h in range(2 * n_pairs)]
    rows = lambda j: pl.ds(pl.multiple_of(j * tk, tk), tk)
    scores = lambda j: [_dot_nt(q[h], _lane_tile(k_ref, rows(j), h)) for h in range(2 * n_pairs)]
    values = lambda j: [_lane_tile(v_ref, rows(j), p) for p in range(n_pairs)]
    outs = _flash_sweep((qi * tq) // tk, scores, values, n_pairs, qi * tq, tq, tk, lo)
    for p in range(n_pairs):
        o_ref[:, p * LANES:(p + 1) * LANES] = outs[p].astype(o_ref.dtype)


def _mla_call(q_mla, k_mla, v_mla, batch, seq, tq, tk, n_pairs, interpret):
    T = q_mla.shape[0]
    nq = seq // tq
    wide, narrow = 2 * n_pairs * LANES, n_pairs * LANES
    return pl.pallas_call(
        functools.partial(_mla_body, tk=tk, n_pairs=n_pairs), grid=(batch, HEAD_PAIRS // n_pairs, nq),
        in_specs=[pl.BlockSpec((tq, wide), lambda b, g, i: (b * nq + i, g)),
                  pl.BlockSpec((seq, wide), lambda b, g, i: (b, g)),
                  pl.BlockSpec((seq, narrow), lambda b, g, i: (b, g))],
        out_specs=pl.BlockSpec((tq, narrow), lambda b, g, i: (b * nq + i, g)),
        out_shape=jax.ShapeDtypeStruct((T, MLA_HEADS * MLA_V), BF16),
        compiler_params=pltpu.CompilerParams(dimension_semantics=("parallel", "parallel", "arbitrary"),
                                             vmem_limit_bytes=VMEM_LIMIT),
        name="mla_attn", interpret=interpret,
    )(q_mla, k_mla, v_mla)


def _top_blocks_bias(gate_t, n_past):
    blk = lax.broadcasted_iota(jnp.int32, gate_t.shape, 0)
    g = jnp.where(blk < n_past, gate_t, -jnp.inf)
    sel = blk == n_past
    for _ in range(MOBA_TOPK):
        mx = jnp.max(g, axis=0, keepdims=True)
        cand = (g == mx) & (mx > -jnp.inf)
        first = jnp.min(jnp.where(cand, blk, LANES), axis=0, keepdims=True)
        pick = blk == first
        sel = sel | pick
        g = jnp.where(pick, -jnp.inf, g)
    return jnp.where(sel, 0.0, NEG_BIG)


def _moba_body(q_ref, k_ref, v_ref, kmean_ref, blk_ref, o_ref, *, group, n_pairs):
    tq = q_ref.shape[0]
    qi = pl.program_id(2)
    lo = lax.broadcasted_iota(jnp.int32, (1, LANES), 1) < MOBA_HD
    n_blocks = kmean_ref.shape[1]

    def aug(qh, kmean):
        gate_t = lax.dot_general(kmean, qh.astype(F32), (((1,), (1,)), ((), ())),
                                 precision=lax.Precision.HIGHEST, preferred_element_type=F32)
        bias_t = jnp.concatenate([_top_blocks_bias(gate_t, qi), jnp.zeros((LANES - n_blocks, tq), F32)], axis=0)
        return jnp.concatenate([qh, bias_t.T.astype(BF16)], axis=1)

    qa = []
    for p in range(n_pairs):
        q = _lane_tile(q_ref, slice(None), p)
        kmean = kmean_ref[0, :, p * LANES:(p + 1) * LANES]
        zq = jnp.zeros_like(q)
        qa += [aug(jnp.where(lo, q, zq), kmean), aug(jnp.where(lo, zq, q), kmean)]
    tk = group * tq
    rows = lambda j: pl.ds(pl.multiple_of(j * tk, tk), tk)

    def scores(j):
        out = []
        for p in range(n_pairs):
            k_aug_t = jnp.concatenate([k_ref[j, p * LANES:(p + 1) * LANES, :], blk_ref[j]], axis=0)
            out += [_dot(qa[2 * p], k_aug_t), _dot(qa[2 * p + 1], k_aug_t)]
        return out

    values = lambda j: [_lane_tile(v_ref, rows(j), p) for p in range(n_pairs)]
    outs = _flash_sweep(qi // group, scores, values, n_pairs, qi * tq, tq, tk, lo)
    for p in range(n_pairs):
        o_ref[:, p * LANES:(p + 1) * LANES] = outs[p].astype(o_ref.dtype)


def _moba_call(q_b, k_b, v_b, kmean, batch, seq, group, n_pairs, interpret):
    T = q_b.shape[0]
    tq = MOBA_BLOCK
    nq = seq // tq
    width = n_pairs * LANES
    tk = group * tq
    n_chunks = seq // tk
    tile = pl.BlockSpec((tq, width), lambda b, g, i: (b * nq + i, g))
    kv = pl.BlockSpec((seq, width), lambda b, g, i: (b, g))
    k_t = pl.BlockSpec((n_chunks, width, tk), lambda b, g, i: (b, g, 0))
    blk_onehot_t = (jnp.arange(LANES, dtype=jnp.int32)[None, :, None]
                    == (jnp.arange(seq, dtype=jnp.int32) // MOBA_BLOCK).reshape(n_chunks, 1, tk)).astype(BF16)
    return pl.pallas_call(
        functools.partial(_moba_body, group=group, n_pairs=n_pairs), grid=(batch, HEAD_PAIRS // n_pairs, nq),
        in_specs=[tile, k_t, kv,
                  pl.BlockSpec((1, nq, width), lambda b, g, i: (b, 0, g)),
                  pl.BlockSpec((n_chunks, LANES, tk), lambda b, g, i: (0, 0, 0))],
        out_specs=tile,
        out_shape=jax.ShapeDtypeStruct((T, MOBA_HEADS * MOBA_HD), BF16),
        compiler_params=pltpu.CompilerParams(dimension_semantics=("parallel", "parallel", "arbitrary"),
                                             vmem_limit_bytes=VMEM_LIMIT),
        name="moba_attn", interpret=interpret,
    )(q_b, k_b, v_b, kmean, blk_onehot_t)


def _slab_rows(ref, row0, n):
    return jnp.concatenate([ref[pl.ds(row0 * ROW_SLABS + s, n, stride=ROW_SLABS), :] for s in range(ROW_SLABS)],
                           axis=1)


def _store_slab_rows(ref, val):
    for s in range(ROW_SLABS):
        ref[pl.ds(s, val.shape[0], stride=ROW_SLABS), :] = val[:, s * LANES:(s + 1) * LANES]


ROUTE_GROUP0 = 0
ROUTE_EXPERT0 = N_GROUPS
ROUTE_EID0, ROUTE_W0, ROUTE_RANK0 = 0, 2, 4


def _first_lane_of_max(vals, lane):
    mx = jnp.max(vals, axis=-1, keepdims=True)
    return mx, jnp.min(jnp.where(vals == mx, lane, LANES), axis=-1, keepdims=True)


def _mix_body(oa_ref, ob_ref, ga_ref, gb_ref, x_ref, woa_ref, wob_ref, wout_ref, fn_ref, wr_ref, br_ref,
              x1_ref, h2_ref, route_ref, counts_ref, carry_ref):
    tm = x_ref.shape[0]

    @pl.when(pl.program_id(0) == 0)
    def _():
        carry_ref[...] = jnp.zeros_like(carry_ref)

    a = _dot(oa_ref[...], woa_ref[...])
    b = _dot(ob_ref[...], wob_ref[...])
    mixed = jax.nn.sigmoid(ga_ref[...].astype(F32)) * a + jax.nn.sigmoid(gb_ref[...].astype(F32)) * b
    x1 = x_ref[...] + _dot(mixed.astype(BF16), wout_ref[...])
    x1_ref[...] = x1
    h2 = _rms(x1, fn_ref[...])
    _store_slab_rows(h2_ref, h2)

    logits = jnp.dot(h2, wr_ref[...], precision=lax.Precision.HIGHEST, preferred_element_type=F32) + br_ref[...]
    lane = lax.broadcasted_iota(jnp.int32, (tm, LANES), 1)
    neg = -jnp.inf
    g_log = jnp.where(lane < N_GROUPS, logits, neg)
    g_max, g_sel = _first_lane_of_max(g_log, lane)
    p_group = 1.0 / jnp.sum(jnp.exp(g_log - g_max), axis=-1, keepdims=True)
    e_lo = ROUTE_EXPERT0 + g_sel * EXPERTS_PER_GROUP
    e_log = jnp.where((lane >= e_lo) & (lane < e_lo + EXPERTS_PER_GROUP), logits, neg)
    v1, i1 = _first_lane_of_max(e_log, lane)
    v2, i2 = _first_lane_of_max(jnp.where(lane == i1, neg, e_log), lane)
    e2 = jnp.exp(v2 - v1)
    w1 = p_group * (1.0 / (1.0 + e2))
    w2 = p_group * (e2 / (1.0 + e2))

    pick1, pick2 = lane == i1, lane == i2
    onehot = (pick1 | pick2).astype(F32)
    tri = (lax.broadcasted_iota(jnp.int32, (tm, tm), 1) < lax.broadcasted_iota(jnp.int32, (tm, tm), 0))
    before = _dot(tri.astype(BF16), onehot.astype(BF16)) + carry_ref[0:1, :]
    r1 = jnp.sum(jnp.where(pick1, before, 0.0), axis=-1, keepdims=True)
    r2 = jnp.sum(jnp.where(pick2, before, 0.0), axis=-1, keepdims=True)
    new_carry = carry_ref[0:1, :] + jnp.sum(onehot, axis=0, keepdims=True)
    carry_ref[...] = jnp.broadcast_to(new_carry, carry_ref.shape)
    counts_ref[...] = jnp.broadcast_to(new_carry, counts_ref.shape)

    fi1 = (i1 - ROUTE_EXPERT0).astype(F32)
    fi2 = (i2 - ROUTE_EXPERT0).astype(F32)
    route = jnp.zeros((tm, LANES), F32)
    for k, val in ((ROUTE_EID0, fi1), (ROUTE_EID0 + 1, fi2), (ROUTE_W0, w1), (ROUTE_W0 + 1, w2),
                   (ROUTE_RANK0, r1), (ROUTE_RANK0 + 1, r2)):
        route = jnp.where(lane == k, val, route)
    route_ref[...] = route


def _mix_call(o_a, o_b, g_a, g_b, x2d, w_oa, w_ob, w_out, ffn_norm, w_router, b_router, tm, interpret):
    T = x2d.shape[0]
    row = lambda i: (i, 0)
    const = lambda i: (0, 0)
    full = lambda a: pl.BlockSpec(a.shape, const, pipeline_mode=pl.Buffered(1))
    return pl.pallas_call(
        _mix_body, grid=(T // tm,),
        in_specs=[pl.BlockSpec((tm, 512), row), pl.BlockSpec((tm, 512), row),
                  pl.BlockSpec((tm, D_MODEL), row), pl.BlockSpec((tm, D_MODEL), row),
                  pl.BlockSpec((tm, D_MODEL), row), full(w_oa), full(w_ob), full(w_out), full(ffn_norm),
                  full(w_router), full(b_router)],
        out_specs=[pl.BlockSpec((tm, D_MODEL), row), pl.BlockSpec((tm * ROW_SLABS, LANES), row),
                   pl.BlockSpec((tm, LANES), row), pl.BlockSpec((8, LANES), const)],
        out_shape=[jax.ShapeDtypeStruct((T, D_MODEL), F32), jax.ShapeDtypeStruct((T * ROW_SLABS, LANES), F32),
                   jax.ShapeDtypeStruct((T, LANES), F32), jax.ShapeDtypeStruct((8, LANES), F32)],
        scratch_shapes=[pltpu.VMEM((8, LANES), F32)],
        compiler_params=pltpu.CompilerParams(dimension_semantics=("arbitrary",), vmem_limit_bytes=VMEM_LIMIT),
        name="mix_route", interpret=interpret,
    )(o_a, o_b, g_a, g_b, x2d, w_oa, w_ob, w_out, ffn_norm, w_router, b_router)


class _RowRing:
    def __init__(self, step, n_steps, src_hbm, idx_ref, idx_next_ref, buf_ref, sems):
        self.step, self.n_steps, self.src = step, n_steps, src_hbm
        self.idx, self.idx_next, self.buf, self.sems = idx_ref, idx_next_ref, buf_ref, sems
        self.slot = step % 2
        self.n_rows = buf_ref.shape[1] // ROW_SLABS

    def _start(self, idx_ref, slot, lo, hi):
        for r in range(lo, hi):
            src = self.src.at[pl.ds(pl.multiple_of(idx_ref[0, 0, r] * ROW_SLABS, ROW_SLABS), ROW_SLABS), :]
            dst = self.buf.at[slot, pl.ds(r * ROW_SLABS, ROW_SLABS), :]
            pltpu.make_async_copy(src, dst, self.sems.at[slot]).start(priority=r % 2)

    def _wait(self, slot):
        pltpu.make_async_copy(self.buf.at[slot], self.buf.at[slot], self.sems.at[slot]).wait()

    def rows(self):
        @pl.when(self.step == 0)
        def _():
            self._start(self.idx, 0, 0, self.n_rows)

        self._wait(self.slot)
        return self.buf.at[self.slot]

    def start_next(self, part, n_parts):
        per = self.n_rows // n_parts
        self._start(self.idx_next, 1 - self.slot, part * per, (part + 1) * per)

    def drain(self):
        @pl.when(self.step == self.n_steps - 1)
        def _():
            self._wait(1 - self.slot)


def _moe_body(blk_e_ref, n_used_ref, tok_ref, tok_next_ref, h_hbm, wg_ref, wu_ref, wd_ref, y_ref, xs_ref, sems):
    del blk_e_ref
    step = pl.program_id(0)
    n_used = n_used_ref[0]

    @pl.when(step < n_used)
    def _():
        ring = _RowRing(step, n_used, h_hbm, tok_ref, tok_next_ref, xs_ref, sems)
        xs = _slab_rows(ring.rows(), 0, ring.n_rows).astype(BF16)
        ring.start_next(0, 4)
        gate = _dot(xs, wg_ref[0])
        ring.start_next(1, 4)
        up = _dot(xs, wu_ref[0])
        ring.start_next(2, 4)
        ys = _dot((jax.nn.silu(gate) * up).astype(BF16), wd_ref[0])
        ring.start_next(3, 4)
        _store_slab_rows(y_ref, ys)
        ring.drain()

    @pl.when(step >= n_used)
    def _():
        y_ref[...] = jnp.zeros_like(y_ref)


def _moe_call(blk_e, n_used, slot_tok, h2, w_g, w_u, w_d, interpret):
    n_blocks = slot_tok.shape[0]
    nb = EXPERT_BLOCK
    tok_spec = lambda off: pl.BlockSpec((1, 1, nb), lambda i, be, nu: (jnp.minimum(i + off, n_blocks - 1), 0, 0),
                                        memory_space=pltpu.SMEM)
    grid_spec = pltpu.PrefetchScalarGridSpec(
        num_scalar_prefetch=2, grid=(n_blocks,),
        in_specs=[tok_spec(0), tok_spec(1),
                  pl.BlockSpec(memory_space=pl.ANY),
                  pl.BlockSpec((1, D_MODEL, EXPERT_FF), lambda i, be, nu: (be[i], 0, 0)),
                  pl.BlockSpec((1, D_MODEL, EXPERT_FF), lambda i, be, nu: (be[i], 0, 0)),
                  pl.BlockSpec((1, EXPERT_FF, D_MODEL), lambda i, be, nu: (be[i], 0, 0))],
        out_specs=pl.BlockSpec((nb * ROW_SLABS, LANES), lambda i, be, nu: (i, 0)),
        scratch_shapes=[pltpu.VMEM((2, nb * ROW_SLABS, LANES), F32), pltpu.SemaphoreType.DMA((2,))],
    )
    return pl.pallas_call(
        _moe_body, grid_spec=grid_spec,
        out_shape=jax.ShapeDtypeStruct((n_blocks * nb * ROW_SLABS, LANES), F32),
        compiler_params=pltpu.CompilerParams(dimension_semantics=("arbitrary",), vmem_limit_bytes=VMEM_LIMIT),
        name="moe_experts", interpret=interpret,
    )(blk_e, n_used, slot_tok, slot_tok, h2, w_g, w_u, w_d)


def _final_body(dest_ref, dest_next_ref, x1_ref, route_ref, y_hbm, g_ref, o_ref, ybuf_ref, sems):
    tm = x1_ref.shape[0]
    ring = _RowRing(pl.program_id(0), pl.num_programs(0), y_hbm, dest_ref, dest_next_ref, ybuf_ref, sems)
    rows = ring.rows()
    n_parts = 4
    rc = tm // n_parts
    for c in range(n_parts):
        sl = pl.ds(c * rc, rc)
        route = route_ref[sl, :]
        lane = lax.broadcasted_iota(jnp.int32, route.shape, 1)
        w1 = jnp.sum(jnp.where(lane == ROUTE_W0, route, 0.0), axis=-1, keepdims=True)
        w2 = jnp.sum(jnp.where(lane == ROUTE_W0 + 1, route, 0.0), axis=-1, keepdims=True)
        y = x1_ref[sl, :] + w1 * _slab_rows(rows, c * rc, rc) + w2 * _slab_rows(rows, tm + c * rc, rc)
        o_ref[sl, :] = _rms(y, g_ref[...])
        ring.start_next(c, n_parts)
    ring.drain()


def _final_call(dest, x1, route, y, final_norm, tm, interpret):
    T = x1.shape[0]
    nt = T // tm
    dest_spec = lambda off: pl.BlockSpec((1, 1, EXPERT_TOPK * tm), lambda i: (jnp.minimum(i + off, nt - 1), 0, 0),
                                         memory_space=pltpu.SMEM)
    return pl.pallas_call(
        _final_body, grid=(nt,),
        in_specs=[dest_spec(0), dest_spec(1),
                  pl.BlockSpec((tm, D_MODEL), lambda i: (i, 0)),
                  pl.BlockSpec((tm, LANES), lambda i: (i, 0)),
                  pl.BlockSpec(memory_space=pl.ANY),
                  pl.BlockSpec((1, D_MODEL), lambda i: (0, 0))],
        out_specs=pl.BlockSpec((tm, D_MODEL), lambda i: (i, 0)),
        out_shape=jax.ShapeDtypeStruct((T, D_MODEL), F32),
        scratch_shapes=[pltpu.VMEM((2, EXPERT_TOPK * tm * ROW_SLABS, LANES), F32), pltpu.SemaphoreType.DMA((2,))],
        compiler_params=pltpu.CompilerParams(dimension_semantics=("arbitrary",), vmem_limit_bytes=VMEM_LIMIT),
        name="final_norm", interpret=interpret,
    )(dest, dest, x1, route, y, final_norm)


def _rope_tables(seq, rot_dim):
    half = rot_dim // 2
    inv_freq = jnp.power(ROPE_THETA, -jnp.arange(half, dtype=F32) / half)
    ang = jnp.arange(seq, dtype=F32)[:, None] * inv_freq[None, :]
    return jnp.cos(ang), jnp.sin(ang)


def _lane_tables(seq):
    cos, sin = _rope_tables(seq, MLA_ROPE)
    z = lambda n: jnp.zeros((seq, n), F32)
    o = lambda n: jnp.ones((seq, n), F32)
    mla = (jnp.concatenate([o(64), cos, cos, o(32)], 1),
           jnp.concatenate([z(64), -sin, z(48)], 1),
           jnp.concatenate([z(80), sin, z(32)], 1))
    cos, sin = _rope_tables(seq, MOBA_ROT)
    head = (jnp.concatenate([cos, cos, o(48)], 1), jnp.concatenate([-sin, z(56)], 1),
            jnp.concatenate([z(8), sin, z(48)], 1))
    moba = tuple(jnp.concatenate([t, t], 1) for t in head)
    return mla + moba


def _prep_weights(w_in, w_uq, w_ukv):
    kr = jnp.pad(w_in[:, 1024:1056], ((0, 0), (64, 32)))
    w_in_r = jnp.concatenate([w_in[:, :1024], kr, w_in[:, 1056:]], axis=1).astype(BF16)
    w_uq_r = jnp.pad(w_uq.reshape(Q_LORA, MLA_HEADS, MLA_NOPE + MLA_ROPE),
                     ((0, 0), (0, 0), (0, LANES - MLA_NOPE - MLA_ROPE))).reshape(Q_LORA, MLA_HEADS * LANES)
    w_kv = w_ukv.reshape(KV_LORA, MLA_HEADS, MLA_NOPE + MLA_V)
    w_uk_r = jnp.pad(w_kv[:, :, :MLA_NOPE], ((0, 0), (0, 0), (0, LANES - MLA_NOPE))).reshape(KV_LORA, -1)
    w_uv_r = w_kv[:, :, MLA_NOPE:].reshape(KV_LORA, MLA_HEADS * MLA_V)
    return w_in_r, w_uq_r.astype(BF16), w_uk_r.astype(BF16), w_uv_r.astype(BF16)


def _router_weights(w_rg, b_rg, w_re, b_re):
    w_e = jnp.transpose(w_re, (1, 0, 2)).reshape(D_MODEL, N_EXPERTS)
    w = jnp.concatenate([w_rg, w_e], axis=1)
    b = jnp.concatenate([b_rg, b_re.reshape(N_EXPERTS)])
    pad = LANES - N_GROUPS - N_EXPERTS
    return jnp.pad(w, ((0, 0), (0, pad))), jnp.pad(b, (0, pad))[None, :]


def _dispatch(route, counts, T, tm):
    eid = route[:, ROUTE_EID0:ROUTE_EID0 + EXPERT_TOPK].astype(jnp.int32)
    rank = route[:, ROUTE_RANK0:ROUTE_RANK0 + EXPERT_TOPK].astype(jnp.int32)
    cnt = counts[0, ROUTE_EXPERT0:ROUTE_EXPERT0 + N_EXPERTS].astype(jnp.int32)
    padded = (cnt + EXPERT_BLOCK - 1) // EXPERT_BLOCK * EXPERT_BLOCK
    pend = jnp.cumsum(padded)
    pstart = pend - padded
    dest = pstart[eid] + rank
    n_blocks = T * EXPERT_TOPK // EXPERT_BLOCK + N_EXPERTS
    n_slots = n_blocks * EXPERT_BLOCK
    tok = jnp.repeat(jnp.arange(T, dtype=jnp.int32), EXPERT_TOPK)
    slot_tok = jnp.zeros((n_slots,), jnp.int32).at[dest.reshape(-1)].set(
        tok, unique_indices=True, mode='promise_in_bounds')
    blk_start = jnp.arange(n_blocks, dtype=jnp.int32) * EXPERT_BLOCK
    blk_e = jnp.minimum(jnp.sum(pend[None, :] <= blk_start[:, None], axis=1), N_EXPERTS - 1).astype(jnp.int32)
    n_used = (pend[-1] // EXPERT_BLOCK).astype(jnp.int32)[None]
    dest_tiles = dest.reshape(T // tm, tm, EXPERT_TOPK).transpose(0, 2, 1).reshape(T // tm, 1, EXPERT_TOPK * tm)
    return blk_e, n_used, slot_tok.reshape(n_blocks, 1, EXPERT_BLOCK), dest_tiles


def _forward(x, attn_norm, w_in, q_norm, w_uq, kv_norm, w_ukv, w_o_mla, w_o_moba, w_out, ffn_norm,
             w_router_group, b_router_group, w_router_expert, b_router_expert, w_exp_gate, w_exp_up,
             w_exp_down, final_norm, *, tm=256, tm_dense=512, tq_mla=512, tk_mla=1024, moba_group=4,
             pairs_per_step=2, interpret=False):
    B, S, _ = x.shape
    T = B * S
    x2d = x.reshape(T, D_MODEL)
    tabs = _lane_tables(S)
    for l in range(attn_norm.shape[0]):
        w_in_r, w_uq_r, w_uk_r, w_uv_r = _prep_weights(w_in[l], w_uq[l], w_ukv[l])
        q_m, k_m, v_m, q_b, k_b, v_b, kmean, g_a, g_b = _proj_call(
            x2d, attn_norm[l][None], w_in_r, q_norm[l][None], w_uq_r, kv_norm[l][None], w_uk_r, w_uv_r,
            tabs, S, tm_dense, moba_group * MOBA_BLOCK, interpret)
        o_a = _mla_call(q_m, k_m, v_m, B, S, tq_mla, tk_mla, pairs_per_step, interpret)
        o_b = _moba_call(q_b, k_b, v_b, kmean.reshape(B, S // MOBA_BLOCK, 512), B, S, moba_group, pairs_per_step, interpret)
        w_r, b_r = _router_weights(w_router_group[l], b_router_group[l], w_router_expert[l], b_router_expert[l])
        x1, h2, route, counts = _mix_call(o_a, o_b, g_a, g_b, x2d, w_o_mla[l].astype(BF16),
                                          w_o_moba[l].astype(BF16), w_out[l].astype(BF16), ffn_norm[l][None],
                                          w_r, b_r, tm_dense, interpret)
        blk_e, n_used, slot_tok, dest_tiles = _dispatch(route, counts, T, tm)
        y = _moe_call(blk_e, n_used, slot_tok, h2, w_exp_gate[l].astype(BF16),
                      w_exp_up[l].astype(BF16), w_exp_down[l].astype(BF16), interpret)
        assert attn_norm.shape[0] == 1
    return _final_call(dest_tiles, x1, route, y, final_norm[None], tm, interpret).reshape(B, S, D_MODEL)


def kernel(x, attn_norm, w_in, q_norm, w_uq, kv_norm, w_ukv, w_o_mla, w_o_moba, w_out, ffn_norm,
           w_router_group, b_router_group, w_router_expert, b_router_expert, w_exp_gate, w_exp_up,
           w_exp_down, final_norm):
    return _forward(x, attn_norm, w_in, q_norm, w_uq, kv_norm, w_ukv, w_o_mla, w_o_moba, w_out, ffn_norm,
                    w_router_group, b_router_group, w_router_expert, b_router_expert, w_exp_gate, w_exp_up,
                    w_exp_down, final_norm)
```

```python
import functools

import jax
import jax.numpy as jnp
import numpy as np
from jax import lax
from jax.experimental import pallas as pl
from jax.experimental.pallas import tpu as pltpu

F32 = jnp.float32
BF16 = jnp.bfloat16

D_MODEL = 1024
ROPE_THETA = 500000.0
NORM_EPS = 1e-6
MLA_HEADS = 8
MLA_NOPE = 64
MLA_ROPE = 32
MLA_V = 64
KV_LORA = 256
Q_LORA = 768
MOBA_HEADS = 8
MOBA_HD = 64
MOBA_ROT = 16
MOBA_BLOCK = 256
MOBA_TOPK = 3
N_GROUPS = 4
EXPERTS_PER_GROUP = 8
N_EXPERTS = 32
EXPERT_TOPK = 2
EXPERT_FF = 256
EXPERT_BLOCK = 256

LANES = 128
ROW_SLABS = D_MODEL // LANES
HEAD_PAIRS = MLA_HEADS // 2
NEG_BIG = -1e30
LOG2_E = 1.4426950408889634
VMEM_LIMIT = 56 * 1024 * 1024

SEG_CQ = (0, 768)
SEG_CKV = (768, 1024)
SEG_KR = (1024, 1152)
SEG_QB = (1152, 1664)
SEG_KB = (1664, 2176)
SEG_VB = (2176, 2688)
SEG_GA = (2688, 3712)
SEG_GB = (3712, 4736)
IN_WIDTH_R = 4736


def _dot(a, b):
    return jnp.dot(a, b, preferred_element_type=F32)


def _dot_nt(a, b):
    return lax.dot_general(a, b, (((1,), (1,)), ((), ())), preferred_element_type=F32)


def _rms(x, g):
    return x * lax.rsqrt(jnp.mean(x * x, axis=-1, keepdims=True) + NORM_EPS) * g


def _rope_tile(x, c, s_lo, s_hi, half):
    return x * c + pltpu.roll(x, LANES - half, 1) * s_lo + pltpu.roll(x, half, 1) * s_hi


def _proj_body(x_ref, g_ref, win_ref, qn_ref, wuq_ref, kvn_ref, wuk_ref, wuv_ref,
               mc_ref, ms1_ref, ms2_ref, bc_ref, bs1_ref, bs2_ref,
               qm_ref, km_ref, vm_ref, qb_ref, kb_ref, vb_ref, kmean_ref, ga_ref, gb_ref):
    tm = x_ref.shape[0]
    h = _rms(x_ref[...], g_ref[...]).astype(BF16)

    def seg(s):
        return _dot(h, win_ref[:, s[0]:s[1]])

    mc, ms1, ms2 = mc_ref[...], ms1_ref[...], ms2_ref[...]
    bc, bs1, bs2 = bc_ref[...], bs1_ref[...], bs2_ref[...]

    hq = _rms(seg(SEG_CQ), qn_ref[...]).astype(BF16)
    q = _dot(hq, wuq_ref[...])
    q_scale = (MLA_NOPE + MLA_ROPE) ** -0.5 * LOG2_E
    for hd in range(MLA_HEADS):
        sl = slice(hd * LANES, (hd + 1) * LANES)
        qm_ref[:, sl] = (_rope_tile(q[:, sl], mc, ms1, ms2, MLA_ROPE // 2) * q_scale).astype(BF16)

    hkv = _rms(seg(SEG_CKV), kvn_ref[...]).astype(BF16)
    k_nope = _dot(hkv, wuk_ref[...])
    kr = _rope_tile(seg(SEG_KR), mc, ms1, ms2, MLA_ROPE // 2)
    for hd in range(MLA_HEADS):
        sl = slice(hd * LANES, (hd + 1) * LANES)
        km_ref[:, sl] = (k_nope[:, sl] + kr).astype(BF16)
    vm_ref[...] = _dot(hkv, wuv_ref[...]).astype(BF16)

    qb = seg(SEG_QB)
    kb = seg(SEG_KB)
    qb_scale = MOBA_HD ** -0.5 * LOG2_E
    n_blk = tm // MOBA_BLOCK
    for t in range(MOBA_HEADS * MOBA_HD // LANES):
        sl = slice(t * LANES, (t + 1) * LANES)
        qb_ref[:, sl] = (_rope_tile(qb[:, sl], bc, bs1, bs2, MOBA_ROT // 2) * qb_scale).astype(BF16)
        kt = _rope_tile(kb[:, sl], bc, bs1, bs2, MOBA_ROT // 2)
        kb_ref[0, sl, :] = kt.T.astype(BF16)
        for j in range(n_blk):
            kmean_ref[j, :, sl] = jnp.mean(kt[j * MOBA_BLOCK:(j + 1) * MOBA_BLOCK], axis=0, keepdims=True)
    vb_ref[...] = seg(SEG_VB).astype(BF16)
    ga_ref[...] = seg(SEG_GA).astype(BF16)
    gb_ref[...] = seg(SEG_GB).astype(BF16)


def _proj_call(x2d, g_attn, w_in_r, q_norm, w_uq_r, kv_norm, w_uk_r, w_uv_r, tabs, seq, tm, moba_tk, interpret):
    T = x2d.shape[0]
    n_seq_tiles = seq // tm
    tiles_per_chunk = moba_tk // tm
    row = lambda i: (i, 0)
    const = lambda i: (0, 0)
    tab = lambda i: (i % n_seq_tiles, 0)
    full = lambda a: pl.BlockSpec(a.shape, const, pipeline_mode=pl.Buffered(1))
    n_blk = tm // MOBA_BLOCK
    out_shapes = [
        jax.ShapeDtypeStruct((T, MLA_HEADS * LANES), BF16),
        jax.ShapeDtypeStruct((T, MLA_HEADS * LANES), BF16),
        jax.ShapeDtypeStruct((T, MLA_HEADS * MLA_V), BF16),
        jax.ShapeDtypeStruct((T, 512), BF16),
        jax.ShapeDtypeStruct((T // moba_tk, 512, moba_tk), BF16),
        jax.ShapeDtypeStruct((T, 512), BF16),
        jax.ShapeDtypeStruct((T // MOBA_BLOCK, 1, 512), F32),
        jax.ShapeDtypeStruct((T, D_MODEL), BF16),
        jax.ShapeDtypeStruct((T, D_MODEL), BF16),
    ]
    out_specs = [
        pl.BlockSpec((tm, 1024), row), pl.BlockSpec((tm, 1024), row), pl.BlockSpec((tm, 512), row),
        pl.BlockSpec((tm, 512), row),
        pl.BlockSpec((1, 512, tm), lambda i: (i // tiles_per_chunk, 0, i % tiles_per_chunk)),
        pl.BlockSpec((tm, 512), row),
        pl.BlockSpec((n_blk, 1, 512), lambda i: (i, 0, 0)),
        pl.BlockSpec((tm, 1024), row), pl.BlockSpec((tm, 1024), row),
    ]
    in_specs = [pl.BlockSpec((tm, D_MODEL), row), full(g_attn), full(w_in_r), full(q_norm), full(w_uq_r),
                full(kv_norm), full(w_uk_r), full(w_uv_r)] + [pl.BlockSpec((tm, LANES), tab)] * 6
    return pl.pallas_call(
        _proj_body, grid=(T // tm,), in_specs=in_specs, out_specs=out_specs, out_shape=out_shapes,
        compiler_params=pltpu.CompilerParams(dimension_semantics=("parallel",), vmem_limit_bytes=VMEM_LIMIT),
        name="proj", interpret=interpret,
    )(x2d, g_attn, w_in_r, q_norm, w_uq_r, kv_norm, w_uk_r, w_uv_r, *tabs)


def _softmax_step(s, m, l):
    m_new = jnp.maximum(m, jnp.max(s, axis=-1, keepdims=True))
    alpha = jnp.exp2(m - m_new)
    p = jnp.exp2(s - m_new)
    return m_new, alpha, alpha * l + jnp.sum(p, axis=-1, keepdims=True), p.astype(BF16)


def _pair_update(state, s0, s1, v, lo):
    m0, l0, m1, l1, acc = state
    m0, a0, l0, p0 = _softmax_step(s0, m0, l0)
    m1, a1, l1, p1 = _softmax_step(s1, m1, l1)
    acc = acc * jnp.where(lo, a0, a1) + jnp.where(lo, _dot(p0, v), _dot(p1, v))
    return m0, l0, m1, l1, acc


def _init_state(tq):
    col = lambda val: jnp.full((tq, 1), val, F32)
    return col(-jnp.inf), col(0.0), col(-jnp.inf), col(0.0), jnp.zeros((tq, LANES), F32)


def _flash_sweep(n_full, scores, values, n_pairs, q_pos0, tq, tk, lo):
    def update(states, s, v):
        return tuple(_pair_update(states[p], s[2 * p], s[2 * p + 1], v[p], lo) for p in range(n_pairs))

    states = lax.fori_loop(0, n_full, lambda j, st: update(st, scores(j), values(j)),
                           tuple(_init_state(tq) for _ in range(n_pairs)))
    kpos = n_full * tk + lax.broadcasted_iota(jnp.int32, (tq, tk), 1)
    causal = kpos <= q_pos0 + lax.broadcasted_iota(jnp.int32, (tq, tk), 0)
    states = update(states, [jnp.where(causal, s, -jnp.inf) for s in scores(n_full)], values(n_full))
    return [acc / jnp.where(lo, l0, l1) for (_, l0, _, l1, acc) in states]


def _lane_tile(ref, rows, t):
    return ref[rows, t * LANES:(t + 1) * LANES]


def _mla_body(q_ref, k_ref, v_ref, o_ref, *, tk, n_pairs):
    tq = q_ref.shape[0]
    qi = pl.program_id(2)
    lo = lax.broadcasted_iota(jnp.int32, (1, LANES), 1) < MLA_V
    q = [_lane_tile(q_ref, slice(None), h) for h in range(2 * n_pairs)]
    rows = lambda j: pl.ds(pl.multiple_of(j * tk, tk), tk)
    scores = lambda j: [_dot_nt(q[h], _lane_tile(k_ref, rows(j), h)) for h in range(2 * n_pairs)]
    values = lambda j: [_lane_tile(v_ref, rows(j), p) for p in range(n_pairs)]
    outs = _flash_sweep((qi * tq) // tk, scores, values, n_pairs, qi * tq, tq, tk, lo)
    for p in range(n_pairs):
        o_ref[:, p * LANES:(p + 1) * LANES] = outs[p].astype(o_ref.dtype)


def _mla_call(q_mla, k_mla, v_mla, batch, seq, tq, tk, n_pairs, interpret):
    T = q_mla.shape[0]
    nq = seq // tq
    wide, narrow = 2 * n_pairs * LANES, n_pairs * LANES
    return pl.pallas_call(
        functools.partial(_mla_body, tk=tk, n_pairs=n_pairs), grid=(batch, HEAD_PAIRS // n_pairs, nq),
        in_specs=[pl.BlockSpec((tq, wide), lambda b, g, i: (b * nq + i, g)),
                  pl.BlockSpec((seq, wide), lambda b, g, i: (b, g)),
                  pl.BlockSpec((seq, narrow), lambda b, g, i: (b, g))],
        out_specs=pl.BlockSpec((tq, narrow), lambda b, g, i: (b * nq + i, g)),
        out_shape=jax.ShapeDtypeStruct((T, MLA_HEADS * MLA_V), BF16),
        compiler_params=pltpu.CompilerParams(dimension_semantics=("parallel", "parallel", "arbitrary"),
                                             vmem_limit_bytes=VMEM_LIMIT),
        name="mla_attn", interpret=interpret,
    )(q_mla, k_mla, v_mla)


def _top_blocks_bias(gate_t, n_past):
    blk = lax.broadcasted_iota(jnp.int32, gate_t.shape, 0)
    g = jnp.where(blk < n_past, gate_t, -jnp.inf)
    sel = blk == n_past
    for _ in range(MOBA_TOPK):
        mx = jnp.max(g, axis=0, keepdims=True)
        cand = (g == mx) & (mx > -jnp.inf)
        first = jnp.min(jnp.where(cand, blk, LANES), axis=0, keepdims=True)
        pick = blk == first
        sel = sel | pick
        g = jnp.where(pick, -jnp.inf, g)
    return jnp.where(sel, 0.0, NEG_BIG)


def _moba_body(q_ref, k_ref, v_ref, kmean_ref, blk_ref, o_ref, *, group, n_pairs):
    tq = q_ref.shape[0]
    qi = pl.program_id(2)
    lo = lax.broadcasted_iota(jnp.int32, (1, LANES), 1) < MOBA_HD
    n_blocks = kmean_ref.shape[1]

    def aug(qh, kmean):
        gate_t = lax.dot_general(kmean, qh.astype(F32), (((1,), (1,)), ((), ())),
                                 precision=lax.Precision.HIGHEST, preferred_element_type=F32)
        bias_t = jnp.concatenate([_top_blocks_bias(gate_t, qi), jnp.zeros((LANES - n_blocks, tq), F32)], axis=0)
        return jnp.concatenate([qh, bias_t.T.astype(BF16)], axis=1)

    qa = []
    for p in range(n_pairs):
        q = _lane_tile(q_ref, slice(None), p)
        kmean = kmean_ref[0, :, p * LANES:(p + 1) * LANES]
        zq = jnp.zeros_like(q)
        qa += [aug(jnp.where(lo, q, zq), kmean), aug(jnp.where(lo, zq, q), kmean)]
    tk = group * tq
    rows = lambda j: pl.ds(pl.multiple_of(j * tk, tk), tk)

    def scores(j):
        out = []
        for p in range(n_pairs):
            k_aug_t = jnp.concatenate([k_ref[j, p * LANES:(p + 1) * LANES, :], blk_ref[j]], axis=0)
            out += [_dot(qa[2 * p], k_aug_t), _dot(qa[2 * p + 1], k_aug_t)]
        return out

    values = lambda j: [_lane_tile(v_ref, rows(j), p) for p in range(n_pairs)]
    outs = _flash_sweep(qi // group, scores, values, n_pairs, qi * tq, tq, tk, lo)
    for p in range(n_pairs):
        o_ref[:, p * LANES:(p + 1) * LANES] = outs[p].astype(o_ref.dtype)


def _moba_call(q_b, k_b, v_b, kmean, batch, seq, group, n_pairs, interpret):
    T = q_b.shape[0]
    tq = MOBA_BLOCK
    nq = seq // tq
    width = n_pairs * LANES
    tk = group * tq
    n_chunks = seq // tk
    tile = pl.BlockSpec((tq, width), lambda b, g, i: (b * nq + i, g))
    kv = pl.BlockSpec((seq, width), lambda b, g, i: (b, g))
    k_t = pl.BlockSpec((n_chunks, width, tk), lambda b, g, i: (b, g, 0))
    blk_onehot_t = (jnp.arange(LANES, dtype=jnp.int32)[None, :, None]
                    == (jnp.arange(seq, dtype=jnp.int32) // MOBA_BLOCK).reshape(n_chunks, 1, tk)).astype(BF16)
    return pl.pallas_call(
        functools.partial(_moba_body, group=group, n_pairs=n_pairs), grid=(batch, HEAD_PAIRS // n_pairs, nq),
        in_specs=[tile, k_t, kv,
                  pl.BlockSpec((1, nq, width), lambda b, g, i: (b, 0, g)),
                  pl.BlockSpec((n_chunks, LANES, tk), lambda b, g, i: (0, 0, 0))],
        out_specs=tile,
        out_shape=jax.ShapeDtypeStruct((T, MOBA_HEADS * MOBA_HD), BF16),
        compiler_params=pltpu.CompilerParams(dimension_semantics=("parallel", "parallel", "arbitrary"),
                                             vmem_limit_bytes=VMEM_LIMIT),
        name="moba_attn", interpret=interpret,
    )(q_b, k_b, v_b, kmean, blk_onehot_t)


def _slab_rows(ref, row0, n):
    return jnp.concatenate([ref[pl.ds(row0 * ROW_SLABS + s, n, stride=ROW_SLABS), :] for s in range(ROW_SLABS)],
                           axis=1)


def _store_slab_rows(ref, val):
    for s in range(ROW_SLABS):
        ref[pl.ds(s, val.shape[0], stride=ROW_SLABS), :] = val[:, s * LANES:(s + 1) * LANES]


ROUTE_GROUP0 = 0
ROUTE_EXPERT0 = N_GROUPS
ROUTE_EID0, ROUTE_W0, ROUTE_RANK0 = 0, 2, 4


def _first_lane_of_max(vals, lane):
    mx = jnp.max(vals, axis=-1, keepdims=True)
    return mx, jnp.min(jnp.where(vals == mx, lane, LANES), axis=-1, keepdims=True)


def _mix_body(oa_ref, ob_ref, ga_ref, gb_ref, x_ref, woa_ref, wob_ref, wout_ref, fn_ref, wr_ref, br_ref,
              x1_ref, h2_ref, route_ref, counts_ref, carry_ref):
    tm = x_ref.shape[0]

    @pl.when(pl.program_id(0) == 0)
    def _():
        carry_ref[...] = jnp.zeros_like(carry_ref)

    a = _dot(oa_ref[...], woa_ref[...])
    b = _dot(ob_ref[...], wob_ref[...])
    mixed = jax.nn.sigmoid(ga_ref[...].astype(F32)) * a + jax.nn.sigmoid(gb_ref[...].astype(F32)) * b
    x1 = x_ref[...] + _dot(mixed.astype(BF16), wout_ref[...])
    x1_ref[...] = x1
    h2 = _rms(x1, fn_ref[...])
    _store_slab_rows(h2_ref, h2)

    logits = jnp.dot(h2, wr_ref[...], precision=lax.Precision.HIGHEST, preferred_element_type=F32) + br_ref[...]
    lane = lax.broadcasted_iota(jnp.int32, (tm, LANES), 1)
    neg = -jnp.inf
    g_log = jnp.where(lane < N_GROUPS, logits, neg)
    g_max, g_sel = _first_lane_of_max(g_log, lane)
    p_group = 1.0 / jnp.sum(jnp.exp(g_log - g_max), axis=-1, keepdims=True)
    e_lo = ROUTE_EXPERT0 + g_sel * EXPERTS_PER_GROUP
    e_log = jnp.where((lane >= e_lo) & (lane < e_lo + EXPERTS_PER_GROUP), logits, neg)
    v1, i1 = _first_lane_of_max(e_log, lane)
    v2, i2 = _first_lane_of_max(jnp.where(lane == i1, neg, e_log), lane)
    e2 = jnp.exp(v2 - v1)
    w1 = p_group * (1.0 / (1.0 + e2))
    w2 = p_group * (e2 / (1.0 + e2))

    pick1, pick2 = lane == i1, lane == i2
    onehot = (pick1 | pick2).astype(F32)
    tri = (lax.broadcasted_iota(jnp.int32, (tm, tm), 1) < lax.broadcasted_iota(jnp.int32, (tm, tm), 0))
    before = _dot(tri.astype(BF16), onehot.astype(BF16)) + carry_ref[0:1, :]
    r1 = jnp.sum(jnp.where(pick1, before, 0.0), axis=-1, keepdims=True)
    r2 = jnp.sum(jnp.where(pick2, before, 0.0), axis=-1, keepdims=True)
    new_carry = carry_ref[0:1, :] + jnp.sum(onehot, axis=0, keepdims=True)
    carry_ref[...] = jnp.broadcast_to(new_carry, carry_ref.shape)
    counts_ref[...] = jnp.broadcast_to(new_carry, counts_ref.shape)

    fi1 = (i1 - ROUTE_EXPERT0).astype(F32)
    fi2 = (i2 - ROUTE_EXPERT0).astype(F32)
    route = jnp.zeros((tm, LANES), F32)
    for k, val in ((ROUTE_EID0, fi1), (ROUTE_EID0 + 1, fi2), (ROUTE_W0, w1), (ROUTE_W0 + 1, w2),
                   (ROUTE_RANK0, r1), (ROUTE_RANK0 + 1, r2)):
        route = jnp.where(lane == k, val, route)
    route_ref[...] = route


def _mix_call(o_a, o_b, g_a, g_b, x2d, w_oa, w_ob, w_out, ffn_norm, w_router, b_router, tm, interpret):
    T = x2d.shape[0]
    row = lambda i: (i, 0)
    const = lambda i: (0, 0)
    full = lambda a: pl.BlockSpec(a.shape, const, pipeline_mode=pl.Buffered(1))
    return pl.pallas_call(
        _mix_body, grid=(T // tm,),
        in_specs=[pl.BlockSpec((tm, 512), row), pl.BlockSpec((tm, 512), row),
                  pl.BlockSpec((tm, D_MODEL), row), pl.BlockSpec((tm, D_MODEL), row),
                  pl.BlockSpec((tm, D_MODEL), row), full(w_oa), full(w_ob), full(w_out), full(ffn_norm),
                  full(w_router), full(b_router)],
        out_specs=[pl.BlockSpec((tm, D_MODEL), row), pl.BlockSpec((tm * ROW_SLABS, LANES), row),
                   pl.BlockSpec((tm, LANES), row), pl.BlockSpec((8, LANES), const)],
        out_shape=[jax.ShapeDtypeStruct((T, D_MODEL), F32), jax.ShapeDtypeStruct((T * ROW_SLABS, LANES), F32),
                   jax.ShapeDtypeStruct((T, LANES), F32), jax.ShapeDtypeStruct((8, LANES), F32)],
        scratch_shapes=[pltpu.VMEM((8, LANES), F32)],
        compiler_params=pltpu.CompilerParams(dimension_semantics=("arbitrary",), vmem_limit_bytes=VMEM_LIMIT),
        name="mix_route", interpret=interpret,
    )(o_a, o_b, g_a, g_b, x2d, w_oa, w_ob, w_out, ffn_norm, w_router, b_router)


class _RowRing:
    def __init__(self, step, n_steps, src_hbm, idx_ref, idx_next_ref, buf_ref, sems):
        self.step, self.n_steps, self.src = step, n_steps, src_hbm
        self.idx, self.idx_next, self.buf, self.sems = idx_ref, idx_next_ref, buf_ref, sems
        self.slot = step % 2
        self.n_rows = buf_ref.shape[1] // ROW_SLABS

    def _start(self, idx_ref, slot, lo, hi):
        for r in range(lo, hi):
            src = self.src.at[pl.ds(pl.multiple_of(idx_ref[0, 0, r] * ROW_SLABS, ROW_SLABS), ROW_SLABS), :]
            dst = self.buf.at[slot, pl.ds(r * ROW_SLABS, ROW_SLABS), :]
            pltpu.make_async_copy(src, dst, self.sems.at[slot]).start(priority=r % 2)

    def _wait(self, slot):
        pltpu.make_async_copy(self.buf.at[slot], self.buf.at[slot], self.sems.at[slot]).wait()

    def rows(self):
        @pl.when(self.step == 0)
        def _():
            self._start(self.idx, 0, 0, self.n_rows)

        self._wait(self.slot)
        return self.buf.at[self.slot]

    def start_next(self, part, n_parts):
        per = self.n_rows // n_parts
        self._start(self.idx_next, 1 - self.slot, part * per, (part + 1) * per)

    def drain(self):
        @pl.when(self.step == self.n_steps - 1)
        def _():
            self._wait(1 - self.slot)


def _moe_body(blk_e_ref, n_used_ref, tok_ref, tok_next_ref, h_hbm, wg_ref, wu_ref, wd_ref, y_ref, xs_ref, sems):
    del blk_e_ref
    step = pl.program_id(0)
    n_used = n_used_ref[0]

    @pl.when(step < n_used)
    def _():
        ring = _RowRing(step, n_used, h_hbm, tok_ref, tok_next_ref, xs_ref, sems)
        xs = _slab_rows(ring.rows(), 0, ring.n_rows).astype(BF16)
        ring.start_next(0, 4)
        gate = _dot(xs, wg_ref[0])
        ring.start_next(1, 4)
        up = _dot(xs, wu_ref[0])
        ring.start_next(2, 4)
        ys = _dot((jax.nn.silu(gate) * up).astype(BF16), wd_ref[0])
        ring.start_next(3, 4)
        _store_slab_rows(y_ref, ys)
        ring.drain()

    @pl.when(step >= n_used)
    def _():
        y_ref[...] = jnp.zeros_like(y_ref)


def _moe_call(blk_e, n_used, slot_tok, h2, w_g, w_u, w_d, interpret):
    n_blocks = slot_tok.shape[0]
    nb = EXPERT_BLOCK
    tok_spec = lambda off: pl.BlockSpec((1, 1, nb), lambda i, be, nu: (jnp.minimum(i + off, n_blocks - 1), 0, 0),
                                        memory_space=pltpu.SMEM)
    grid_spec = pltpu.PrefetchScalarGridSpec(
        num_scalar_prefetch=2, grid=(n_blocks,),
        in_specs=[tok_spec(0), tok_spec(1),
                  pl.BlockSpec(memory_space=pl.ANY),
                  pl.BlockSpec((1, D_MODEL, EXPERT_FF), lambda i, be, nu: (be[i], 0, 0)),
                  pl.BlockSpec((1, D_MODEL, EXPERT_FF), lambda i, be, nu: (be[i], 0, 0)),
                  pl.BlockSpec((1, EXPERT_FF, D_MODEL), lambda i, be, nu: (be[i], 0, 0))],
        out_specs=pl.BlockSpec((nb * ROW_SLABS, LANES), lambda i, be, nu: (i, 0)),
        scratch_shapes=[pltpu.VMEM((2, nb * ROW_SLABS, LANES), F32), pltpu.SemaphoreType.DMA((2,))],
    )
    return pl.pallas_call(
        _moe_body, grid_spec=grid_spec,
        out_shape=jax.ShapeDtypeStruct((n_blocks * nb * ROW_SLABS, LANES), F32),
        compiler_params=pltpu.CompilerParams(dimension_semantics=("arbitrary",), vmem_limit_bytes=VMEM_LIMIT),
        name="moe_experts", interpret=interpret,
    )(blk_e, n_used, slot_tok, slot_tok, h2, w_g, w_u, w_d)


def _final_body(dest_ref, dest_next_ref, x1_ref, route_ref, y_hbm, g_ref, o_ref, ybuf_ref, sems):
    tm = x1_ref.shape[0]
    ring = _RowRing(pl.program_id(0), pl.num_programs(0), y_hbm, dest_ref, dest_next_ref, ybuf_ref, sems)
    rows = ring.rows()
    rc = 64
    n_parts = tm // rc
    for c in range(n_parts):
        sl = pl.ds(c * rc, rc)
        route = route_ref[sl, :]
        lane = lax.broadcasted_iota(jnp.int32, route.shape, 1)
        w1 = jnp.sum(jnp.where(lane == ROUTE_W0, route, 0.0), axis=-1, keepdims=True)
        w2 = jnp.sum(jnp.where(lane == ROUTE_W0 + 1, route, 0.0), axis=-1, keepdims=True)
        y = x1_ref[sl, :] + w1 * _slab_rows(rows, c * rc, rc) + w2 * _slab_rows(rows, tm + c * rc, rc)
        o_ref[sl, :] = _rms(y, g_ref[...])
        ring.start_next(c, n_parts)
    ring.drain()


def _final_call(dest, x1, route, y, final_norm, tm, interpret):
    T = x1.shape[0]
    nt = T // tm
    dest_spec = lambda off: pl.BlockSpec((1, 1, EXPERT_TOPK * tm), lambda i: (jnp.minimum(i + off, nt - 1), 0, 0),
                                         memory_space=pltpu.SMEM)
    return pl.pallas_call(
        _final_body, grid=(nt,),
        in_specs=[dest_spec(0), dest_spec(1),
                  pl.BlockSpec((tm, D_MODEL), lambda i: (i, 0)),
                  pl.BlockSpec((tm, LANES), lambda i: (i, 0)),
                  pl.BlockSpec(memory_space=pl.ANY),
                  pl.BlockSpec((1, D_MODEL), lambda i: (0, 0))],
        out_specs=pl.BlockSpec((tm, D_MODEL), lambda i: (i, 0)),
        out_shape=jax.ShapeDtypeStruct((T, D_MODEL), F32),
        scratch_shapes=[pltpu.VMEM((2, EXPERT_TOPK * tm * ROW_SLABS, LANES), F32), pltpu.SemaphoreType.DMA((2,))],
        compiler_params=pltpu.CompilerParams(dimension_semantics=("arbitrary",), vmem_limit_bytes=VMEM_LIMIT),
        name="final_norm", interpret=interpret,
    )(dest, dest, x1, route, y, final_norm)


def _rope_tables(seq, rot_dim):
    half = rot_dim // 2
    inv_freq = jnp.power(ROPE_THETA, -jnp.arange(half, dtype=F32) / half)
    ang = jnp.arange(seq, dtype=F32)[:, None] * inv_freq[None, :]
    return jnp.cos(ang), jnp.sin(ang)


def _lane_tables(seq):
    cos, sin = _rope_tables(seq, MLA_ROPE)
    z = lambda n: jnp.zeros((seq, n), F32)
    o = lambda n: jnp.ones((seq, n), F32)
    mla = (jnp.concatenate([o(64), cos, cos, o(32)], 1),
           jnp.concatenate([z(64), -sin, z(48)], 1),
           jnp.concatenate([z(80), sin, z(32)], 1))
    cos, sin = _rope_tables(seq, MOBA_ROT)
    head = (jnp.concatenate([cos, cos, o(48)], 1), jnp.concatenate([-sin, z(56)], 1),
            jnp.concatenate([z(8), sin, z(48)], 1))
    moba = tuple(jnp.concatenate([t, t], 1) for t in head)
    return mla + moba


def _prep_weights(w_in, w_uq, w_ukv):
    kr = jnp.pad(w_in[:, 1024:1056], ((0, 0), (64, 32)))
    w_in_r = jnp.concatenate([w_in[:, :1024], kr, w_in[:, 1056:]], axis=1).astype(BF16)
    w_uq_r = jnp.pad(w_uq.reshape(Q_LORA, MLA_HEADS, MLA_NOPE + MLA_ROPE),
                     ((0, 0), (0, 0), (0, LANES - MLA_NOPE - MLA_ROPE))).reshape(Q_LORA, MLA_HEADS * LANES)
    w_kv = w_ukv.reshape(KV_LORA, MLA_HEADS, MLA_NOPE + MLA_V)
    w_uk_r = jnp.pad(w_kv[:, :, :MLA_NOPE], ((0, 0), (0, 0), (0, LANES - MLA_NOPE))).reshape(KV_LORA, -1)
    w_uv_r = w_kv[:, :, MLA_NOPE:].reshape(KV_LORA, MLA_HEADS * MLA_V)
    return w_in_r, w_uq_r.astype(BF16), w_uk_r.astype(BF16), w_uv_r.astype(BF16)


def _router_weights(w_rg, b_rg, w_re, b_re):
    w_e = jnp.transpose(w_re, (1, 0, 2)).reshape(D_MODEL, N_EXPERTS)
    w = jnp.concatenate([w_rg, w_e], axis=1)
    b = jnp.concatenate([b_rg, b_re.reshape(N_EXPERTS)])
    pad = LANES - N_GROUPS - N_EXPERTS
    return jnp.pad(w, ((0, 0), (0, pad))), jnp.pad(b, (0, pad))[None, :]


def _dispatch(route, counts, T, tm):
    eid = route[:, ROUTE_EID0:ROUTE_EID0 + EXPERT_TOPK].astype(jnp.int32)
    rank = route[:, ROUTE_RANK0:ROUTE_RANK0 + EXPERT_TOPK].astype(jnp.int32)
    cnt = counts[0, ROUTE_EXPERT0:ROUTE_EXPERT0 + N_EXPERTS].astype(jnp.int32)
    padded = (cnt + EXPERT_BLOCK - 1) // EXPERT_BLOCK * EXPERT_BLOCK
    pend = jnp.cumsum(padded)
    pstart = pend - padded
    dest = pstart[eid] + rank
    n_blocks = T * EXPERT_TOPK // EXPERT_BLOCK + N_EXPERTS
    n_slots = n_blocks * EXPERT_BLOCK
    tok = jnp.repeat(jnp.arange(T, dtype=jnp.int32), EXPERT_TOPK)
    slot_tok = jnp.zeros((n_slots,), jnp.int32).at[dest.reshape(-1)].set(
        tok, unique_indices=True, mode='promise_in_bounds')
    blk_start = jnp.arange(n_blocks, dtype=jnp.int32) * EXPERT_BLOCK
    blk_e = jnp.minimum(jnp.sum(pend[None, :] <= blk_start[:, None], axis=1), N_EXPERTS - 1).astype(jnp.int32)
    n_used = (pend[-1] // EXPERT_BLOCK).astype(jnp.int32)[None]
    dest_tiles = dest.reshape(T // tm, tm, EXPERT_TOPK).transpose(0, 2, 1).reshape(T // tm, 1, EXPERT_TOPK * tm)
    return blk_e, n_used, slot_tok.reshape(n_blocks, 1, EXPERT_BLOCK), dest_tiles


def _forward(x, attn_norm, w_in, q_norm, w_uq, kv_norm, w_ukv, w_o_mla, w_o_moba, w_out, ffn_norm,
             w_router_group, b_router_group, w_router_expert, b_router_expert, w_exp_gate, w_exp_up,
             w_exp_down, final_norm, *, tm=512, tm_dense=512, tq_mla=512, tk_mla=1024, moba_group=4,
             pairs_per_step=2, interpret=False):
    B, S, _ = x.shape
    T = B * S
    x2d = x.reshape(T, D_MODEL)
    tabs = _lane_tables(S)
    for l in range(attn_norm.shape[0]):
        w_in_r, w_uq_r, w_uk_r, w_uv_r = _prep_weights(w_in[l], w_uq[l], w_ukv[l])
        q_m, k_m, v_m, q_b, k_b, v_b, kmean, g_a, g_b = _proj_call(
            x2d, attn_norm[l][None], w_in_r, q_norm[l][None], w_uq_r, kv_norm[l][None], w_uk_r, w_uv_r,
            tabs, S, tm_dense, moba_group * MOBA_BLOCK, interpret)
        o_a = _mla_call(q_m, k_m, v_m, B, S, tq_mla, tk_mla, pairs_per_step, interpret)
        o_b = _moba_call(q_b, k_b, v_b, kmean.reshape(B, S // MOBA_BLOCK, 512), B, S, moba_group, pairs_per_step, interpret)
        w_r, b_r = _router_weights(w_router_group[l], b_router_group[l], w_router_expert[l], b_router_expert[l])
        x1, h2, route, counts = _mix_call(o_a, o_b, g_a, g_b, x2d, w_o_mla[l].astype(BF16),
                                          w_o_moba[l].astype(BF16), w_out[l].astype(BF16), ffn_norm[l][None],
                                          w_r, b_r, tm_dense, interpret)
        blk_e, n_used, slot_tok, dest_tiles = _dispatch(route, counts, T, tm)
        y = _moe_call(blk_e, n_used, slot_tok, h2, w_exp_gate[l].astype(BF16),
                      w_exp_up[l].astype(BF16), w_exp_down[l].astype(BF16), interpret)
        assert attn_norm.shape[0] == 1
    return _final_call(dest_tiles, x1, route, y, final_norm[None], tm, interpret).reshape(B, S, D_MODEL)


def kernel(x, attn_norm, w_in, q_norm, w_uq, kv_norm, w_ukv, w_o_mla, w_o_moba, w_out, ffn_norm,
           w_router_group, b_router_group, w_router_expert, b_router_expert, w_exp_gate, w_exp_up,
           w_exp_down, final_norm):
    return _forward(x, attn_norm, w_in, q_norm, w_uq, kv_norm, w_ukv, w_o_mla, w_o_moba, w_out, ffn_norm,
                    w_router_group, b_router_group, w_router_expert, b_router_expert, w_exp_gate, w_exp_up,
                    w_exp_down, final_norm)
```

```python
import functools

import jax
import jax.numpy as jnp
import numpy as np
from jax import lax
from jax.experimental import pallas as pl
from jax.experimental.pallas import tpu as pltpu

F32 = jnp.float32
BF16 = jnp.bfloat16

D_MODEL = 1024
ROPE_THETA = 500000.0
NORM_EPS = 1e-6
MLA_HEADS = 8
MLA_NOPE = 64
MLA_ROPE = 32
MLA_V = 64
KV_LORA = 256
Q_LORA = 768
MOBA_HEADS = 8
MOBA_HD = 64
MOBA_ROT = 16
MOBA_BLOCK = 256
MOBA_TOPK = 3
N_GROUPS = 4
EXPERTS_PER_GROUP = 8
N_EXPERTS = 32
EXPERT_TOPK = 2
EXPERT_FF = 256
EXPERT_BLOCK = 256

LANES = 128
ROW_SLABS = D_MODEL // LANES
HEAD_PAIRS = MLA_HEADS // 2
NEG_BIG = -1e30
LOG2_E = 1.4426950408889634
VMEM_LIMIT = 56 * 1024 * 1024

SEG_CQ = (0, 768)
SEG_CKV = (768, 1024)
SEG_KR = (1024, 1152)
SEG_QB = (1152, 1664)
SEG_KB = (1664, 2176)
SEG_VB = (2176, 2688)
SEG_GA = (2688, 3712)
SEG_GB = (3712, 4736)
IN_WIDTH_R = 4736


def _dot(a, b):
    return jnp.dot(a, b, preferred_element_type=F32)


def _dot_nt(a, b):
    return lax.dot_general(a, b, (((1,), (1,)), ((), ())), preferred_element_type=F32)


def _rms(x, g):
    return x * lax.rsqrt(jnp.mean(x * x, axis=-1, keepdims=True) + NORM_EPS) * g


def _rope_tile(x, c, s_lo, s_hi, half):
    return x * c + pltpu.roll(x, LANES - half, 1) * s_lo + pltpu.roll(x, half, 1) * s_hi


def _proj_body(x_ref, g_ref, win_ref, qn_ref, wuq_ref, kvn_ref, wuk_ref, wuv_ref,
               mc_ref, ms1_ref, ms2_ref, bc_ref, bs1_ref, bs2_ref,
               qm_ref, km_ref, vm_ref, qb_ref, kb_ref, vb_ref, kmean_ref, ga_ref, gb_ref):
    tm = x_ref.shape[0]
    h = _rms(x_ref[...], g_ref[...]).astype(BF16)

    def seg(s):
        return _dot(h, win_ref[:, s[0]:s[1]])

    mc, ms1, ms2 = mc_ref[...], ms1_ref[...], ms2_ref[...]
    bc, bs1, bs2 = bc_ref[...], bs1_ref[...], bs2_ref[...]

    hq = _rms(seg(SEG_CQ), qn_ref[...]).astype(BF16)
    q = _dot(hq, wuq_ref[...])
    q_scale = (MLA_NOPE + MLA_ROPE) ** -0.5 * LOG2_E
    for hd in range(MLA_HEADS):
        sl = slice(hd * LANES, (hd + 1) * LANES)
        qm_ref[:, sl] = (_rope_tile(q[:, sl], mc, ms1, ms2, MLA_ROPE // 2) * q_scale).astype(BF16)

    hkv = _rms(seg(SEG_CKV), kvn_ref[...]).astype(BF16)
    k_nope = _dot(hkv, wuk_ref[...])
    kr = _rope_tile(seg(SEG_KR), mc, ms1, ms2, MLA_ROPE // 2)
    for hd in range(MLA_HEADS):
        sl = slice(hd * LANES, (hd + 1) * LANES)
        km_ref[:, sl] = (k_nope[:, sl] + kr).astype(BF16)
    vm_ref[...] = _dot(hkv, wuv_ref[...]).astype(BF16)

    qb = seg(SEG_QB)
    kb = seg(SEG_KB)
    qb_scale = MOBA_HD ** -0.5 * LOG2_E
    n_blk = tm // MOBA_BLOCK
    for t in range(MOBA_HEADS * MOBA_HD // LANES):
        sl = slice(t * LANES, (t + 1) * LANES)
        qb_ref[:, sl] = (_rope_tile(qb[:, sl], bc, bs1, bs2, MOBA_ROT // 2) * qb_scale).astype(BF16)
        kt = _rope_tile(kb[:, sl], bc, bs1, bs2, MOBA_ROT // 2)
        kb_ref[0, sl, :] = kt.T.astype(BF16)
        for j in range(n_blk):
            kmean_ref[j, :, sl] = jnp.mean(kt[j * MOBA_BLOCK:(j + 1) * MOBA_BLOCK], axis=0, keepdims=True)
    vb_ref[...] = seg(SEG_VB).astype(BF16)
    ga_ref[...] = seg(SEG_GA).astype(BF16)
    gb_ref[...] = seg(SEG_GB).astype(BF16)


def _proj_call(x2d, g_attn, w_in_r, q_norm, w_uq_r, kv_norm, w_uk_r, w_uv_r, tabs, seq, tm, moba_tk, interpret):
    T = x2d.shape[0]
    n_seq_tiles = seq // tm
    tiles_per_chunk = moba_tk // tm
    row = lambda i: (i, 0)
    const = lambda i: (0, 0)
    tab = lambda i: (i % n_seq_tiles, 0)
    full = lambda a: pl.BlockSpec(a.shape, const, pipeline_mode=pl.Buffered(1))
    n_blk = tm // MOBA_BLOCK
    out_shapes = [
        jax.ShapeDtypeStruct((T, MLA_HEADS * LANES), BF16),
        jax.ShapeDtypeStruct((T, MLA_HEADS * LANES), BF16),
        jax.ShapeDtypeStruct((T, MLA_HEADS * MLA_V), BF16),
        jax.ShapeDtypeStruct((T, 512), BF16),
        jax.ShapeDtypeStruct((T // moba_tk, 512, moba_tk), BF16),
        jax.ShapeDtypeStruct((T, 512), BF16),
        jax.ShapeDtypeStruct((T // MOBA_BLOCK, 1, 512), F32),
        jax.ShapeDtypeStruct((T, D_MODEL), BF16),
        jax.ShapeDtypeStruct((T, D_MODEL), BF16),
    ]
    out_specs = [
        pl.BlockSpec((tm, 1024), row), pl.BlockSpec((tm, 1024), row), pl.BlockSpec((tm, 512), row),
        pl.BlockSpec((tm, 512), row),
        pl.BlockSpec((1, 512, tm), lambda i: (i // tiles_per_chunk, 0, i % tiles_per_chunk)),
        pl.BlockSpec((tm, 512), row),
        pl.BlockSpec((n_blk, 1, 512), lambda i: (i, 0, 0)),
        pl.BlockSpec((tm, 1024), row), pl.BlockSpec((tm, 1024), row),
    ]
    in_specs = [pl.BlockSpec((tm, D_MODEL), row), full(g_attn), full(w_in_r), full(q_norm), full(w_uq_r),
                full(kv_norm), full(w_uk_r), full(w_uv_r)] + [pl.BlockSpec((tm, LANES), tab)] * 6
    return pl.pallas_call(
        _proj_body, grid=(T // tm,), in_specs=in_specs, out_specs=out_specs, out_shape=out_shapes,
        compiler_params=pltpu.CompilerParams(dimension_semantics=("parallel",), vmem_limit_bytes=VMEM_LIMIT),
        name="proj", interpret=interpret,
    )(x2d, g_attn, w_in_r, q_norm, w_uq_r, kv_norm, w_uk_r, w_uv_r, *tabs)


def _softmax_step(s, m, l):
    m_new = jnp.maximum(m, jnp.max(s, axis=-1, keepdims=True))
    alpha = jnp.exp2(m - m_new)
    p = jnp.exp2(s - m_new)
    return m_new, alpha, alpha * l + jnp.sum(p, axis=-1, keepdims=True), p.astype(BF16)


def _pair_update(state, s0, s1, v, lo):
    m0, l0, m1, l1, acc = state
    m0, a0, l0, p0 = _softmax_step(s0, m0, l0)
    m1, a1, l1, p1 = _softmax_step(s1, m1, l1)
    acc = acc * jnp.where(lo, a0, a1) + jnp.where(lo, _dot(p0, v), _dot(p1, v))
    return m0, l0, m1, l1, acc


def _init_state(tq):
    col = lambda val: jnp.full((tq, 1), val, F32)
    return col(-jnp.inf), col(0.0), col(-jnp.inf), col(0.0), jnp.zeros((tq, LANES), F32)


def _flash_sweep(n_full, scores, values, n_pairs, q_pos0, tq, tk, lo):
    def update(states, s, v):
        return tuple(_pair_update(states[p], s[2 * p], s[2 * p + 1], v[p], lo) for p in range(n_pairs))

    states = lax.fori_loop(0, n_full, lambda j, st: update(st, scores(j), values(j)),
                           tuple(_init_state(tq) for _ in range(n_pairs)))
    kpos = n_full * tk + lax.broadcasted_iota(jnp.int32, (tq, tk), 1)
    causal = kpos <= q_pos0 + lax.broadcasted_iota(jnp.int32, (tq, tk), 0)
    states = update(states, [jnp.where(causal, s, -jnp.inf) for s in scores(n_full)], values(n_full))
    return [acc / jnp.where(lo, l0, l1) for (_, l0, _, l1, acc) in states]


def _lane_tile(ref, rows, t):
    return ref[rows, t * LANES:(t + 1) * LANES]


def _mla_body(q_ref, k_ref, v_ref, o_ref, *, tk, n_pairs):
    tq = q_ref.shape[0]
    qi = pl.program_id(2)
    lo = lax.broadcasted_iota(jnp.int32, (1, LANES), 1) < MLA_V
    q = [_lane_tile(q_ref, slice(None), h) for h in range(2 * n_pairs)]
    rows = lambda j: pl.ds(pl.multiple_of(j * tk, tk), tk)
    scores = lambda j: [_dot_nt(q[h], _lane_tile(k_ref, rows(j), h)) for h in range(2 * n_pairs)]
    values = lambda j: [_lane_tile(v_ref, rows(j), p) for p in range(n_pairs)]
    outs = _flash_sweep((qi * tq) // tk, scores, values, n_pairs, qi * tq, tq, tk, lo)
    for p in range(n_pairs):
        o_ref[:, p * LANES:(p + 1) * LANES] = outs[p].astype(o_ref.dtype)


def _mla_call(q_mla, k_mla, v_mla, batch, seq, tq, tk, n_pairs, interpret):
    T = q_mla.shape[0]
    nq = seq // tq
    wide, narrow = 2 * n_pairs * LANES, n_pairs * LANES
    return pl.pallas_call(
        functools.partial(_mla_body, tk=tk, n_pairs=n_pairs), grid=(batch, HEAD_PAIRS // n_pairs, nq),
        in_specs=[pl.BlockSpec((tq, wide), lambda b, g, i: (b * nq + i, g)),
                  pl.BlockSpec((seq, wide), lambda b, g, i: (b, g)),
                  pl.BlockSpec((seq, narrow), lambda b, g, i: (b, g))],
        out_specs=pl.BlockSpec((tq, narrow), lambda b, g, i: (b * nq + i, g)),
        out_shape=jax.ShapeDtypeStruct((T, MLA_HEADS * MLA_V), BF16),
        compiler_params=pltpu.CompilerParams(dimension_semantics=("parallel", "parallel", "arbitrary"),
                                             vmem_limit_bytes=VMEM_LIMIT),
        name="mla_attn", interpret=interpret,
    )(q_mla, k_mla, v_mla)


def _top_blocks_bias(gate_t, n_past):
    blk = lax.broadcasted_iota(jnp.int32, gate_t.shape, 0)
    g = jnp.where(blk < n_past, gate_t, -jnp.inf)
    sel = blk == n_past
    for _ in range(MOBA_TOPK):
        mx = jnp.max(g, axis=0, keepdims=True)
        cand = (g == mx) & (mx > -jnp.inf)
        first = jnp.min(jnp.where(cand, blk, LANES), axis=0, keepdims=True)
        pick = blk == first
        sel = sel | pick
        g = jnp.where(pick, -jnp.inf, g)
    return jnp.where(sel, 0.0, NEG_BIG)


def _moba_body(q_ref, k_ref, v_ref, kmean_ref, blk_ref, o_ref, *, group, n_pairs):
    tq = q_ref.shape[0]
    qi = pl.program_id(2)
    lo = lax.broadcasted_iota(jnp.int32, (1, LANES), 1) < MOBA_HD
    n_blocks = kmean_ref.shape[1]

    def aug(qh, kmean):
        gate_t = lax.dot_general(kmean, qh.astype(F32), (((1,), (1,)), ((), ())),
                                 precision=lax.Precision.HIGHEST, preferred_element_type=F32)
        bias_t = jnp.concatenate([_top_blocks_bias(gate_t, qi), jnp.zeros((LANES - n_blocks, tq), F32)], axis=0)
        return jnp.concatenate([qh, bias_t.T.astype(BF16)], axis=1)

    qa = []
    for p in range(n_pairs):
        q = _lane_tile(q_ref, slice(None), p)
        kmean = kmean_ref[0, :, p * LANES:(p + 1) * LANES]
        zq = jnp.zeros_like(q)
        qa += [aug(jnp.where(lo, q, zq), kmean), aug(jnp.where(lo, zq, q), kmean)]
    tk = group * tq
    rows = lambda j: pl.ds(pl.multiple_of(j * tk, tk), tk)

    def scores(j):
        out = []
        for p in range(n_pairs):
            k_aug_t = jnp.concatenate([k_ref[j, p * LANES:(p + 1) * LANES, :], blk_ref[j]], axis=0)
            out += [_dot(qa[2 * p], k_aug_t), _dot(qa[2 * p + 1], k_aug_t)]
        return out

    values = lambda j: [_lane_tile(v_ref, rows(j), p) for p in range(n_pairs)]
    outs = _flash_sweep(qi // group, scores, values, n_pairs, qi * tq, tq, tk, lo)
    for p in range(n_pairs):
        o_ref[:, p * LANES:(p + 1) * LANES] = outs[p].astype(o_ref.dtype)


def _moba_call(q_b, k_b, v_b, kmean, batch, seq, group, n_pairs, interpret):
    T = q_b.shape[0]
    tq = MOBA_BLOCK
    nq = seq // tq
    width = n_pairs * LANES
    tk = group * tq
    n_chunks = seq // tk
    tile = pl.BlockSpec((tq, width), lambda b, g, i: (b * nq + i, g))
    kv = pl.BlockSpec((seq, width), lambda b, g, i: (b, g))
    k_t = pl.BlockSpec((n_chunks, width, tk), lambda b, g, i: (b, g, 0))
    blk_onehot_t = (jnp.arange(LANES, dtype=jnp.int32)[None, :, None]
                    == (jnp.arange(seq, dtype=jnp.int32) // MOBA_BLOCK).reshape(n_chunks, 1, tk)).astype(BF16)
    return pl.pallas_call(
        functools.partial(_moba_body, group=group, n_pairs=n_pairs), grid=(batch, HEAD_PAIRS // n_pairs, nq),
        in_specs=[tile, k_t, kv,
                  pl.BlockSpec((1, nq, width), lambda b, g, i: (b, 0, g)),
                  pl.BlockSpec((n_chunks, LANES, tk), lambda b, g, i: (0, 0, 0))],
        out_specs=tile,
        out_shape=jax.ShapeDtypeStruct((T, MOBA_HEADS * MOBA_HD), BF16),
        compiler_params=pltpu.CompilerParams(dimension_semantics=("parallel", "parallel", "arbitrary"),
                                             vmem_limit_bytes=VMEM_LIMIT),
        name="moba_attn", interpret=interpret,
    )(q_b, k_b, v_b, kmean, blk_onehot_t)


def _slab_rows(ref, row0, n):
    return jnp.concatenate([ref[pl.ds(row0 * ROW_SLABS + s, n, stride=ROW_SLABS), :] for s in range(ROW_SLABS)],
                           axis=1)


def _store_slab_rows(ref, val):
    for s in range(ROW_SLABS):
        ref[pl.ds(s, val.shape[0], stride=ROW_SLABS), :] = val[:, s * LANES:(s + 1) * LANES]


ROUTE_GROUP0 = 0
ROUTE_EXPERT0 = N_GROUPS
ROUTE_EID0, ROUTE_W0, ROUTE_RANK0 = 0, 2, 4


def _first_lane_of_max(vals, lane):
    mx = jnp.max(vals, axis=-1, keepdims=True)
    return mx, jnp.min(jnp.where(vals == mx, lane, LANES), axis=-1, keepdims=True)


def _mix_body(oa_ref, ob_ref, ga_ref, gb_ref, x_ref, woa_ref, wob_ref, wout_ref, fn_ref, wr_ref, br_ref,
              x1_ref, h2_ref, route_ref, counts_ref, carry_ref):
    tm = x_ref.shape[0]

    @pl.when(pl.program_id(0) == 0)
    def _():
        carry_ref[...] = jnp.zeros_like(carry_ref)

    a = _dot(oa_ref[...], woa_ref[...])
    b = _dot(ob_ref[...], wob_ref[...])
    mixed = jax.nn.sigmoid(ga_ref[...].astype(F32)) * a + jax.nn.sigmoid(gb_ref[...].astype(F32)) * b
    x1 = x_ref[...] + _dot(mixed.astype(BF16), wout_ref[...])
    x1_ref[...] = x1
    h2 = _rms(x1, fn_ref[...])
    _store_slab_rows(h2_ref, h2)

    logits = jnp.dot(h2, wr_ref[...], precision=lax.Precision.HIGHEST, preferred_element_type=F32) + br_ref[...]
    lane = lax.broadcasted_iota(jnp.int32, (tm, LANES), 1)
    neg = -jnp.inf
    g_log = jnp.where(lane < N_GROUPS, logits, neg)
    g_max, g_sel = _first_lane_of_max(g_log, lane)
    p_group = 1.0 / jnp.sum(jnp.exp(g_log - g_max), axis=-1, keepdims=True)
    e_lo = ROUTE_EXPERT0 + g_sel * EXPERTS_PER_GROUP
    e_log = jnp.where((lane >= e_lo) & (lane < e_lo + EXPERTS_PER_GROUP), logits, neg)
    v1, i1 = _first_lane_of_max(e_log, lane)
    v2, i2 = _first_lane_of_max(jnp.where(lane == i1, neg, e_log), lane)
    e2 = jnp.exp(v2 - v1)
    w1 = p_group * (1.0 / (1.0 + e2))
    w2 = p_group * (e2 / (1.0 + e2))

    pick1, pick2 = lane == i1, lane == i2
    onehot = (pick1 | pick2).astype(F32)
    tri = (lax.broadcasted_iota(jnp.int32, (tm, tm), 1) < lax.broadcasted_iota(jnp.int32, (tm, tm), 0))
    before = _dot(tri.astype(BF16), onehot.astype(BF16)) + carry_ref[0:1, :]
    r1 = jnp.sum(jnp.where(pick1, before, 0.0), axis=-1, keepdims=True)
    r2 = jnp.sum(jnp.where(pick2, before, 0.0), axis=-1, keepdims=True)
    new_carry = carry_ref[0:1, :] + jnp.sum(onehot, axis=0, keepdims=True)
    carry_ref[...] = jnp.broadcast_to(new_carry, carry_ref.shape)
    counts_ref[...] = jnp.broadcast_to(new_carry, counts_ref.shape)

    fi1 = (i1 - ROUTE_EXPERT0).astype(F32)
    fi2 = (i2 - ROUTE_EXPERT0).astype(F32)
    route = jnp.zeros((tm, LANES), F32)
    for k, val in ((ROUTE_EID0, fi1), (ROUTE_EID0 + 1, fi2), (ROUTE_W0, w1), (ROUTE_W0 + 1, w2),
                   (ROUTE_RANK0, r1), (ROUTE_RANK0 + 1, r2)):
        route = jnp.where(lane == k, val, route)
    route_ref[...] = route


def _mix_call(o_a, o_b, g_a, g_b, x2d, w_oa, w_ob, w_out, ffn_norm, w_router, b_router, tm, interpret):
    T = x2d.shape[0]
    row = lambda i: (i, 0)
    const = lambda i: (0, 0)
    full = lambda a: pl.BlockSpec(a.shape, const, pipeline_mode=pl.Buffered(1))
    return pl.pallas_call(
        _mix_body, grid=(T // tm,),
        in_specs=[pl.BlockSpec((tm, 512), row), pl.BlockSpec((tm, 512), row),
                  pl.BlockSpec((tm, D_MODEL), row), pl.BlockSpec((tm, D_MODEL), row),
                  pl.BlockSpec((tm, D_MODEL), row), full(w_oa), full(w_ob), full(w_out), full(ffn_norm),
                  full(w_router), full(b_router)],
        out_specs=[pl.BlockSpec((tm, D_MODEL), row), pl.BlockSpec((tm * ROW_SLABS, LANES), row),
                   pl.BlockSpec((tm, LANES), row), pl.BlockSpec((8, LANES), const)],
        out_shape=[jax.ShapeDtypeStruct((T, D_MODEL), F32), jax.ShapeDtypeStruct((T * ROW_SLABS, LANES), F32),
                   jax.ShapeDtypeStruct((T, LANES), F32), jax.ShapeDtypeStruct((8, LANES), F32)],
        scratch_shapes=[pltpu.VMEM((8, LANES), F32)],
        compiler_params=pltpu.CompilerParams(dimension_semantics=("arbitrary",), vmem_limit_bytes=VMEM_LIMIT),
        name="mix_route", interpret=interpret,
    )(o_a, o_b, g_a, g_b, x2d, w_oa, w_ob, w_out, ffn_norm, w_router, b_router)


class _RowRing:
    def __init__(self, step, n_steps, src_hbm, idx_ref, idx_next_ref, buf_ref, sems):
        self.step, self.n_steps, self.src = step, n_steps, src_hbm
        self.idx, self.idx_next, self.buf, self.sems = idx_ref, idx_next_ref, buf_ref, sems
        self.slot = step % 2
        self.n_rows = buf_ref.shape[1] // ROW_SLABS

    def _start(self, idx_ref, slot, lo, hi):
        for r in range(lo, hi):
            src = self.src.at[pl.ds(pl.multiple_of(idx_ref[0, 0, r] * ROW_SLABS, ROW_SLABS), ROW_SLABS), :]
            dst = self.buf.at[slot, pl.ds(r * ROW_SLABS, ROW_SLABS), :]
            pltpu.make_async_copy(src, dst, self.sems.at[slot]).start(priority=r % 2)

    def _wait(self, slot):
        pltpu.make_async_copy(self.buf.at[slot], self.buf.at[slot], self.sems.at[slot]).wait()

    def rows(self):
        @pl.when(self.step == 0)
        def _():
            self._start(self.idx, 0, 0, self.n_rows)

        self._wait(self.slot)
        return self.buf.at[self.slot]

    def start_next(self, part, n_parts):
        per = self.n_rows // n_parts
        self._start(self.idx_next, 1 - self.slot, part * per, (part + 1) * per)

    def drain(self):
        @pl.when(self.step == self.n_steps - 1)
        def _():
            self._wait(1 - self.slot)


def _moe_body(blk_e_ref, n_used_ref, tok_ref, tok_next_ref, h_hbm, wg_ref, wu_ref, wd_ref, y_ref, xs_ref, sems):
    del blk_e_ref
    step = pl.program_id(0)
    n_used = n_used_ref[0]

    @pl.when(step < n_used)
    def _():
        ring = _RowRing(step, n_used, h_hbm, tok_ref, tok_next_ref, xs_ref, sems)
        xs = _slab_rows(ring.rows(), 0, ring.n_rows).astype(BF16)
        ring.start_next(0, 4)
        gate = _dot(xs, wg_ref[0])
        ring.start_next(1, 4)
        up = _dot(xs, wu_ref[0])
        ring.start_next(2, 4)
        ys = _dot((jax.nn.silu(gate) * up).astype(BF16), wd_ref[0])
        ring.start_next(3, 4)
        _store_slab_rows(y_ref, ys)
        ring.drain()

    @pl.when(step >= n_used)
    def _():
        y_ref[...] = jnp.zeros_like(y_ref)


def _moe_call(blk_e, n_used, slot_tok, h2, w_g, w_u, w_d, interpret):
    n_blocks = slot_tok.shape[0]
    nb = EXPERT_BLOCK
    tok_spec = lambda off: pl.BlockSpec((1, 1, nb), lambda i, be, nu: (jnp.minimum(i + off, n_blocks - 1), 0, 0),
                                        memory_space=pltpu.SMEM)
    grid_spec = pltpu.PrefetchScalarGridSpec(
        num_scalar_prefetch=2, grid=(n_blocks,),
        in_specs=[tok_spec(0), tok_spec(1),
                  pl.BlockSpec(memory_space=pl.ANY),
                  pl.BlockSpec((1, D_MODEL, EXPERT_FF), lambda i, be, nu: (be[i], 0, 0)),
                  pl.BlockSpec((1, D_MODEL, EXPERT_FF), lambda i, be, nu: (be[i], 0, 0)),
                  pl.BlockSpec((1, EXPERT_FF, D_MODEL), lambda i, be, nu: (be[i], 0, 0))],
        out_specs=pl.BlockSpec((nb * ROW_SLABS, LANES), lambda i, be, nu: (i, 0)),
        scratch_shapes=[pltpu.VMEM((2, nb * ROW_SLABS, LANES), F32), pltpu.SemaphoreType.DMA((2,))],
    )
    return pl.pallas_call(
        _moe_body, grid_spec=grid_spec,
        out_shape=jax.ShapeDtypeStruct((n_blocks * nb * ROW_SLABS, LANES), F32),
        compiler_params=pltpu.CompilerParams(dimension_semantics=("arbitrary",), vmem_limit_bytes=VMEM_LIMIT),
        name="moe_experts", interpret=interpret,
    )(blk_e, n_used, slot_tok, slot_tok, h2, w_g, w_u, w_d)


def _final_body(dest_ref, dest_next_ref, x1_ref, route_ref, y_hbm, g_ref, o_ref, ybuf_ref, sems):
    tm = x1_ref.shape[0]
    ring = _RowRing(pl.program_id(0), pl.num_programs(0), y_hbm, dest_ref, dest_next_ref, ybuf_ref, sems)
    rows = ring.rows()
    rc = 64
    n_parts = tm // rc
    for c in range(n_parts):
        sl = pl.ds(c * rc, rc)
        route = route_ref[sl, :]
        lane = lax.broadcasted_iota(jnp.int32, route.shape, 1)
        w1 = jnp.sum(jnp.where(lane == ROUTE_W0, route, 0.0), axis=-1, keepdims=True)
        w2 = jnp.sum(jnp.where(lane == ROUTE_W0 + 1, route, 0.0), axis=-1, keepdims=True)
        y = x1_ref[sl, :] + w1 * _slab_rows(rows, c * rc, rc) + w2 * _slab_rows(rows, tm + c * rc, rc)
        o_ref[sl, :] = _rms(y, g_ref[...])
        ring.start_next(c, n_parts)
    ring.drain()


def _final_call(dest, x1, route, y, final_norm, tm, interpret):
    T = x1.shape[0]
    nt = T // tm
    dest_spec = lambda off: pl.BlockSpec((1, 1, EXPERT_TOPK * tm), lambda i: (jnp.minimum(i + off, nt - 1), 0, 0),
                                         memory_space=pltpu.SMEM)
    return pl.pallas_call(
        _final_body, grid=(nt,),
        in_specs=[dest_spec(0), dest_spec(1),
                  pl.BlockSpec((tm, D_MODEL), lambda i: (i, 0)),
                  pl.BlockSpec((tm, LANES), lambda i: (i, 0)),
                  pl.BlockSpec(memory_space=pl.ANY),
                  pl.BlockSpec((1, D_MODEL), lambda i: (0, 0))],
        out_specs=pl.BlockSpec((tm, D_MODEL), lambda i: (i, 0)),
        out_shape=jax.ShapeDtypeStruct((T, D_MODEL), F32),
        scratch_shapes=[pltpu.VMEM((2, EXPERT_TOPK * tm * ROW_SLABS, LANES), F32), pltpu.SemaphoreType.DMA((2,))],
        compiler_params=pltpu.CompilerParams(dimension_semantics=("arbitrary",), vmem_limit_bytes=VMEM_LIMIT),
        name="final_norm", interpret=interpret,
    )(dest, dest, x1, route, y, final_norm)


def _rope_tables(seq, rot_dim):
    half = rot_dim // 2
    inv_freq = jnp.power(ROPE_THETA, -jnp.arange(half, dtype=F32) / half)
    ang = jnp.arange(seq, dtype=F32)[:, None] * inv_freq[None, :]
    return jnp.cos(ang), jnp.sin(ang)


def _lane_tables(seq):
    cos, sin = _rope_tables(seq, MLA_ROPE)
    z = lambda n: jnp.zeros((seq, n), F32)
    o = lambda n: jnp.ones((seq, n), F32)
    mla = (jnp.concatenate([o(64), cos, cos, o(32)], 1),
           jnp.concatenate([z(64), -sin, z(48)], 1),
           jnp.concatenate([z(80), sin, z(32)], 1))
    cos, sin = _rope_tables(seq, MOBA_ROT)
    head = (jnp.concatenate([cos, cos, o(48)], 1), jnp.concatenate([-sin, z(56)], 1),
            jnp.concatenate([z(8), sin, z(48)], 1))
    moba = tuple(jnp.concatenate([t, t], 1) for t in head)
    return mla + moba


def _prep_weights(w_in, w_uq, w_ukv):
    kr = jnp.pad(w_in[:, 1024:1056], ((0, 0), (64, 32)))
    w_in_r = jnp.concatenate([w_in[:, :1024], kr, w_in[:, 1056:]], axis=1).astype(BF16)
    w_uq_r = jnp.pad(w_uq.reshape(Q_LORA, MLA_HEADS, MLA_NOPE + MLA_ROPE),
                     ((0, 0), (0, 0), (0, LANES - MLA_NOPE - MLA_ROPE))).reshape(Q_LORA, MLA_HEADS * LANES)
    w_kv = w_ukv.reshape(KV_LORA, MLA_HEADS, MLA_NOPE + MLA_V)
    w_uk_r = jnp.pad(w_kv[:, :, :MLA_NOPE], ((0, 0), (0, 0), (0, LANES - MLA_NOPE))).reshape(KV_LORA, -1)
    w_uv_r = w_kv[:, :, MLA_NOPE:].reshape(KV_LORA, MLA_HEADS * MLA_V)
    return w_in_r, w_uq_r.astype(BF16), w_uk_r.astype(BF16), w_uv_r.astype(BF16)


def _router_weights(w_rg, b_rg, w_re, b_re):
    w_e = jnp.transpose(w_re, (1, 0, 2)).reshape(D_MODEL, N_EXPERTS)
    w = jnp.concatenate([w_rg, w_e], axis=1)
    b = jnp.concatenate([b_rg, b_re.reshape(N_EXPERTS)])
    pad = LANES - N_GROUPS - N_EXPERTS
    return jnp.pad(w, ((0, 0), (0, pad))), jnp.pad(b, (0, pad))[None, :]


def _dispatch(route, counts, T, tm):
    eid = route[:, ROUTE_EID0:ROUTE_EID0 + EXPERT_TOPK].astype(jnp.int32)
    rank = route[:, ROUTE_RANK0:ROUTE_RANK0 + EXPERT_TOPK].astype(jnp.int32)
    cnt = counts[0, ROUTE_EXPERT0:ROUTE_EXPERT0 + N_EXPERTS].astype(jnp.int32)
    padded = (cnt + EXPERT_BLOCK - 1) // EXPERT_BLOCK * EXPERT_BLOCK
    pend = jnp.cumsum(padded)
    pstart = pend - padded
    dest = pstart[eid] + rank
    n_blocks = T * EXPERT_TOPK // EXPERT_BLOCK + N_EXPERTS
    n_slots = n_blocks * EXPERT_BLOCK
    tok = jnp.repeat(jnp.arange(T, dtype=jnp.int32), EXPERT_TOPK)
    slot_tok = jnp.zeros((n_slots,), jnp.int32).at[dest.reshape(-1)].set(
        tok, unique_indices=True, mode='promise_in_bounds')
    blk_start = jnp.arange(n_blocks, dtype=jnp.int32) * EXPERT_BLOCK
    blk_e = jnp.minimum(jnp.sum(pend[None, :] <= blk_start[:, None], axis=1), N_EXPERTS - 1).astype(jnp.int32)
    n_used = (pend[-1] // EXPERT_BLOCK).astype(jnp.int32)[None]
    dest_tiles = dest.reshape(T // tm, tm, EXPERT_TOPK).transpose(0, 2, 1).reshape(T // tm, 1, EXPERT_TOPK * tm)
    return blk_e, n_used, slot_tok.reshape(n_blocks, 1, EXPERT_BLOCK), dest_tiles


def _forward(x, attn_norm, w_in, q_norm, w_uq, kv_norm, w_ukv, w_o_mla, w_o_moba, w_out, ffn_norm,
             w_router_group, b_router_group, w_router_expert, b_router_expert, w_exp_gate, w_exp_up,
             w_exp_down, final_norm, *, tm=1024, tm_dense=512, tq_mla=512, tk_mla=1024, moba_group=4,
             pairs_per_step=2, interpret=False):
    B, S, _ = x.shape
    T = B * S
    x2d = x.reshape(T, D_MODEL)
    tabs = _lane_tables(S)
    for l in range(attn_norm.shape[0]):
        w_in_r, w_uq_r, w_uk_r, w_uv_r = _prep_weights(w_in[l], w_uq[l], w_ukv[l])
        q_m, k_m, v_m, q_b, k_b, v_b, kmean, g_a, g_b = _proj_call(
            x2d, attn_norm[l][None], w_in_r, q_norm[l][None], w_uq_r, kv_norm[l][None], w_uk_r, w_uv_r,
            tabs, S, tm_dense, moba_group * MOBA_BLOCK, interpret)
        o_a = _mla_call(q_m, k_m, v_m, B, S, tq_mla, tk_mla, pairs_per_step, interpret)
        o_b = _moba_call(q_b, k_b, v_b, kmean.reshape(B, S // MOBA_BLOCK, 512), B, S, moba_group, pairs_per_step, interpret)
        w_r, b_r = _router_weights(w_router_group[l], b_router_group[l], w_router_expert[l], b_router_expert[l])
        x1, h2, route, counts = _mix_call(o_a, o_b, g_a, g_b, x2d, w_o_mla[l].astype(BF16),
                                          w_o_moba[l].astype(BF16), w_out[l].astype(BF16), ffn_norm[l][None],
                                          w_r, b_r, tm_dense, interpret)
        blk_e, n_used, slot_tok, dest_tiles = _dispatch(route, counts, T, tm)
        y = _moe_call(blk_e, n_used, slot_tok, h2, w_exp_gate[l].astype(BF16),
                      w_exp_up[l].astype(BF16), w_exp_down[l].astype(BF16), interpret)
        assert attn_norm.shape[0] == 1
    return _final_call(dest_tiles, x1, route, y, final_norm[None], tm, interpret).reshape(B, S, D_MODEL)


def kernel(x, attn_norm, w_in, q_norm, w_uq, kv_norm, w_ukv, w_o_mla, w_o_moba, w_out, ffn_norm,
           w_router_group, b_router_group, w_router_expert, b_router_expert, w_exp_gate, w_exp_up,
           w_exp_down, final_norm):
    return _forward(x, attn_norm, w_in, q_norm, w_uq, kv_norm, w_ukv, w_o_mla, w_o_moba, w_out, ffn_norm,
                    w_router_group, b_router_group, w_router_expert, b_router_expert, w_exp_gate, w_exp_up,
                    w_exp_down, final_norm)
```

```python
import functools

import jax
import jax.numpy as jnp
import numpy as np
from jax import lax
from jax.experimental import pallas as pl
from jax.experimental.pallas import tpu as pltpu

F32 = jnp.float32
BF16 = jnp.bfloat16

D_MODEL = 1024
ROPE_THETA = 500000.0
NORM_EPS = 1e-6
MLA_HEADS = 8
MLA_NOPE = 64
MLA_ROPE = 32
MLA_V = 64
KV_LORA = 256
Q_LORA = 768
MOBA_HEADS = 8
MOBA_HD = 64
MOBA_ROT = 16
MOBA_BLOCK = 256
MOBA_TOPK = 3
N_GROUPS = 4
EXPERTS_PER_GROUP = 8
N_EXPERTS = 32
EXPERT_TOPK = 2
EXPERT_FF = 256
EXPERT_BLOCK = 256

LANES = 128
ROW_SLABS = D_MODEL // LANES
HEAD_PAIRS = MLA_HEADS // 2
NEG_BIG = -1e30
LOG2_E = 1.4426950408889634
VMEM_LIMIT = 56 * 1024 * 1024

SEG_CQ = (0, 768)
SEG_CKV = (768, 1024)
SEG_KR = (1024, 1152)
SEG_QB = (1152, 1664)
SEG_KB = (1664, 2176)
SEG_VB = (2176, 2688)
SEG_GA = (2688, 3712)
SEG_GB = (3712, 4736)
IN_WIDTH_R = 4736


def _dot(a, b):
    return jnp.dot(a, b, preferred_element_type=F32)


def _dot_nt(a, b):
    return lax.dot_general(a, b, (((1,), (1,)), ((), ())), preferred_element_type=F32)


def _rms(x, g):
    return x * lax.rsqrt(jnp.mean(x * x, axis=-1, keepdims=True) + NORM_EPS) * g


def _rope_tile(x, c, s_lo, s_hi, half):
    return x * c + pltpu.roll(x, LANES - half, 1) * s_lo + pltpu.roll(x, half, 1) * s_hi


def _proj_body(x_ref, g_ref, win_ref, qn_ref, wuq_ref, kvn_ref, wuk_ref, wuv_ref,
               mc_ref, ms1_ref, ms2_ref, bc_ref, bs1_ref, bs2_ref,
               qm_ref, km_ref, vm_ref, qb_ref, kb_ref, vb_ref, kmean_ref, ga_ref, gb_ref):
    tm = x_ref.shape[0]
    h = _rms(x_ref[...], g_ref[...]).astype(BF16)

    def seg(s):
        return _dot(h, win_ref[:, s[0]:s[1]])

    mc, ms1, ms2 = mc_ref[...], ms1_ref[...], ms2_ref[...]
    bc, bs1, bs2 = bc_ref[...], bs1_ref[...], bs2_ref[...]

    hq = _rms(seg(SEG_CQ), qn_ref[...]).astype(BF16)
    q = _dot(hq, wuq_ref[...])
    q_scale = (MLA_NOPE + MLA_ROPE) ** -0.5 * LOG2_E
    for hd in range(MLA_HEADS):
        sl = slice(hd * LANES, (hd + 1) * LANES)
        qm_ref[:, sl] = (_rope_tile(q[:, sl], mc, ms1, ms2, MLA_ROPE // 2) * q_scale).astype(BF16)

    hkv = _rms(seg(SEG_CKV), kvn_ref[...]).astype(BF16)
    k_nope = _dot(hkv, wuk_ref[...])
    kr = _rope_tile(seg(SEG_KR), mc, ms1, ms2, MLA_ROPE // 2)
    for hd in range(MLA_HEADS):
        sl = slice(hd * LANES, (hd + 1) * LANES)
        km_ref[:, sl] = (k_nope[:, sl] + kr).astype(BF16)
    vm_ref[...] = _dot(hkv, wuv_ref[...]).astype(BF16)

    qb = seg(SEG_QB)
    kb = seg(SEG_KB)
    qb_scale = MOBA_HD ** -0.5 * LOG2_E
    n_blk = tm // MOBA_BLOCK
    for t in range(MOBA_HEADS * MOBA_HD // LANES):
        sl = slice(t * LANES, (t + 1) * LANES)
        qb_ref[:, sl] = (_rope_tile(qb[:, sl], bc, bs1, bs2, MOBA_ROT // 2) * qb_scale).astype(BF16)
        kt = _rope_tile(kb[:, sl], bc, bs1, bs2, MOBA_ROT // 2)
        kb_ref[0, sl, :] = kt.T.astype(BF16)
        for j in range(n_blk):
            kmean_ref[j, :, sl] = jnp.mean(kt[j * MOBA_BLOCK:(j + 1) * MOBA_BLOCK], axis=0, keepdims=True)
    vb_ref[...] = seg(SEG_VB).astype(BF16)
    ga_ref[...] = seg(SEG_GA).astype(BF16)
    gb_ref[...] = seg(SEG_GB).astype(BF16)


def _proj_call(x2d, g_attn, w_in_r, q_norm, w_uq_r, kv_norm, w_uk_r, w_uv_r, tabs, seq, tm, moba_tk, interpret):
    T = x2d.shape[0]
    n_seq_tiles = seq // tm
    tiles_per_chunk = moba_tk // tm
    row = lambda i: (i, 0)
    const = lambda i: (0, 0)
    tab = lambda i: (i % n_seq_tiles, 0)
    full = lambda a: pl.BlockSpec(a.shape, const, pipeline_mode=pl.Buffered(1))
    n_blk = tm // MOBA_BLOCK
    out_shapes = [
        jax.ShapeDtypeStruct((T, MLA_HEADS * LANES), BF16),
        jax.ShapeDtypeStruct((T, MLA_HEADS * LANES), BF16),
        jax.ShapeDtypeStruct((T, MLA_HEADS * MLA_V), BF16),
        jax.ShapeDtypeStruct((T, 512), BF16),
        jax.ShapeDtypeStruct((T // moba_tk, 512, moba_tk), BF16),
        jax.ShapeDtypeStruct((T, 512), BF16),
        jax.ShapeDtypeStruct((T // MOBA_BLOCK, 1, 512), F32),
        jax.ShapeDtypeStruct((T, D_MODEL), BF16),
        jax.ShapeDtypeStruct((T, D_MODEL), BF16),
    ]
    out_specs = [
        pl.BlockSpec((tm, 1024), row), pl.BlockSpec((tm, 1024), row), pl.BlockSpec((tm, 512), row),
        pl.BlockSpec((tm, 512), row),
        pl.BlockSpec((1, 512, tm), lambda i: (i // tiles_per_chunk, 0, i % tiles_per_chunk)),
        pl.BlockSpec((tm, 512), row),
        pl.BlockSpec((n_blk, 1, 512), lambda i: (i, 0, 0)),
        pl.BlockSpec((tm, 1024), row), pl.BlockSpec((tm, 1024), row),
    ]
    in_specs = [pl.BlockSpec((tm, D_MODEL), row), full(g_attn), full(w_in_r), full(q_norm), full(w_uq_r),
                full(kv_norm), full(w_uk_r), full(w_uv_r)] + [pl.BlockSpec((tm, LANES), tab)] * 6
    return pl.pallas_call(
        _proj_body, grid=(T // tm,), in_specs=in_specs, out_specs=out_specs, out_shape=out_shapes,
        compiler_params=pltpu.CompilerParams(dimension_semantics=("parallel",), vmem_limit_bytes=VMEM_LIMIT),
        name="proj", interpret=interpret,
    )(x2d, g_attn, w_in_r, q_norm, w_uq_r, kv_norm, w_uk_r, w_uv_r, *tabs)


def _softmax_step(s, m, l):
    m_new = jnp.maximum(m, jnp.max(s, axis=-1, keepdims=True))
    alpha = jnp.exp2(m - m_new)
    p = jnp.exp2(s - m_new)
    return m_new, alpha, alpha * l + jnp.sum(p, axis=-1, keepdims=True), p.astype(BF16)


def _pair_update(state, s0, s1, v, lo):
    m0, l0, m1, l1, acc = state
    m0, a0, l0, p0 = _softmax_step(s0, m0, l0)
    m1, a1, l1, p1 = _softmax_step(s1, m1, l1)
    acc = acc * jnp.where(lo, a0, a1) + jnp.where(lo, _dot(p0, v), _dot(p1, v))
    return m0, l0, m1, l1, acc


def _init_state(tq):
    col = lambda val: jnp.full((tq, 1), val, F32)
    return col(-jnp.inf), col(0.0), col(-jnp.inf), col(0.0), jnp.zeros((tq, LANES), F32)


def _flash_sweep(n_full, scores, values, n_pairs, q_pos0, tq, tk, lo):
    def update(states, s, v):
        return tuple(_pair_update(states[p], s[2 * p], s[2 * p + 1], v[p], lo) for p in range(n_pairs))

    states = lax.fori_loop(0, n_full, lambda j, st: update(st, scores(j), values(j)),
                           tuple(_init_state(tq) for _ in range(n_pairs)))
    kpos = n_full * tk + lax.broadcasted_iota(jnp.int32, (tq, tk), 1)
    causal = kpos <= q_pos0 + lax.broadcasted_iota(jnp.int32, (tq, tk), 0)
    states = update(states, [jnp.where(causal, s, -jnp.inf) for s in scores(n_full)], values(n_full))
    return [acc / jnp.where(lo, l0, l1) for (_, l0, _, l1, acc) in states]


def _lane_tile(ref, rows, t):
    return ref[rows, t * LANES:(t + 1) * LANES]


def _mla_body(q_ref, k_ref, v_ref, o_ref, *, tk, n_pairs):
    tq = q_ref.shape[0]
    qi = pl.program_id(2)
    lo = lax.broadcasted_iota(jnp.int32, (1, LANES), 1) < MLA_V
    q = [_lane_tile(q_ref, slice(None), h) for h in range(2 * n_pairs)]
    rows = lambda j: pl.ds(pl.multiple_of(j * tk, tk), tk)
    scores = lambda j: [_dot_nt(q[h], _lane_tile(k_ref, rows(j), h)) for h in range(2 * n_pairs)]
    values = lambda j: [_lane_tile(v_ref, rows(j), p) for p in range(n_pairs)]
    outs = _flash_sweep((qi * tq) // tk, scores, values, n_pairs, qi * tq, tq, tk, lo)
    for p in range(n_pairs):
        o_ref[:, p * LANES:(p + 1) * LANES] = outs[p].astype(o_ref.dtype)


def _mla_call(q_mla, k_mla, v_mla, batch, seq, tq, tk, n_pairs, interpret):
    T = q_mla.shape[0]
    nq = seq // tq
    wide, narrow = 2 * n_pairs * LANES, n_pairs * LANES
    return pl.pallas_call(
        functools.partial(_mla_body, tk=tk, n_pairs=n_pairs), grid=(batch, HEAD_PAIRS // n_pairs, nq),
        in_specs=[pl.BlockSpec((tq, wide), lambda b, g, i: (b * nq + i, g)),
                  pl.BlockSpec((seq, wide), lambda b, g, i: (b, g)),
                  pl.BlockSpec((seq, narrow), lambda b, g, i: (b, g))],
        out_specs=pl.BlockSpec((tq, narrow), lambda b, g, i: (b * nq + i, g)),
        out_shape=jax.ShapeDtypeStruct((T, MLA_HEADS * MLA_V), BF16),
        compiler_params=pltpu.CompilerParams(dimension_semantics=("parallel", "parallel", "arbitrary"),
                                             vmem_limit_bytes=VMEM_LIMIT),
        name="mla_attn", interpret=interpret,
    )(q_mla, k_mla, v_mla)


def _top_blocks_bias(gate_t, n_past):
    blk = lax.broadcasted_iota(jnp.int32, gate_t.shape, 0)
    g = jnp.where(blk < n_past, gate_t, -jnp.inf)
    sel = blk == n_past
    for _ in range(MOBA_TOPK):
        mx = jnp.max(g, axis=0, keepdims=True)
        cand = (g == mx) & (mx > -jnp.inf)
        first = jnp.min(jnp.where(cand, blk, LANES), axis=0, keepdims=True)
        pick = blk == first
        sel = sel | pick
        g = jnp.where(pick, -jnp.inf, g)
    return jnp.where(sel, 0.0, NEG_BIG)


def _moba_body(q_ref, k_ref, v_ref, kmean_ref, blk_ref, o_ref, *, group, n_pairs):
    tq = q_ref.shape[0]
    qi = pl.program_id(2)
    lo = lax.broadcasted_iota(jnp.int32, (1, LANES), 1) < MOBA_HD
    n_blocks = kmean_ref.shape[1]

    def aug(qh, kmean):
        gate_t = lax.dot_general(kmean, qh.astype(F32), (((1,), (1,)), ((), ())),
                                 precision=lax.Precision.HIGHEST, preferred_element_type=F32)
        bias_t = jnp.concatenate([_top_blocks_bias(gate_t, qi), jnp.zeros((LANES - n_blocks, tq), F32)], axis=0)
        return jnp.concatenate([qh, bias_t.T.astype(BF16)], axis=1)

    qa = []
    for p in range(n_pairs):
        q = _lane_tile(q_ref, slice(None), p)
        kmean = kmean_ref[0, :, p * LANES:(p + 1) * LANES]
        zq = jnp.zeros_like(q)
        qa += [aug(jnp.where(lo, q, zq), kmean), aug(jnp.where(lo, zq, q), kmean)]
    tk = group * tq
    rows = lambda j: pl.ds(pl.multiple_of(j * tk, tk), tk)

    def scores(j):
        out = []
        for p in range(n_pairs):
            k_aug_t = jnp.concatenate([k_ref[j, p * LANES:(p + 1) * LANES, :], blk_ref[j]], axis=0)
            out += [_dot(qa[2 * p], k_aug_t), _dot(qa[2 * p + 1], k_aug_t)]
        return out

    values = lambda j: [_lane_tile(v_ref, rows(j), p) for p in range(n_pairs)]
    outs = _flash_sweep(qi // group, scores, values, n_pairs, qi * tq, tq, tk, lo)
    for p in range(n_pairs):
        o_ref[:, p * LANES:(p + 1) * LANES] = outs[p].astype(o_ref.dtype)


def _moba_call(q_b, k_b, v_b, kmean, batch, seq, group, n_pairs, interpret):
    T = q_b.shape[0]
    tq = MOBA_BLOCK
    nq = seq // tq
    width = n_pairs * LANES
    tk = group * tq
    n_chunks = seq // tk
    tile = pl.BlockSpec((tq, width), lambda b, g, i: (b * nq + i, g))
    kv = pl.BlockSpec((seq, width), lambda b, g, i: (b, g))
    k_t = pl.BlockSpec((n_chunks, width, tk), lambda b, g, i: (b, g, 0))
    blk_onehot_t = (jnp.arange(LANES, dtype=jnp.int32)[None, :, None]
                    == (jnp.arange(seq, dtype=jnp.int32) // MOBA_BLOCK).reshape(n_chunks, 1, tk)).astype(BF16)
    return pl.pallas_call(
        functools.partial(_moba_body, group=group, n_pairs=n_pairs), grid=(batch, HEAD_PAIRS // n_pairs, nq),
        in_specs=[tile, k_t, kv,
                  pl.BlockSpec((1, nq, width), lambda b, g, i: (b, 0, g)),
                  pl.BlockSpec((n_chunks, LANES, tk), lambda b, g, i: (0, 0, 0))],
        out_specs=tile,
        out_shape=jax.ShapeDtypeStruct((T, MOBA_HEADS * MOBA_HD), BF16),
        compiler_params=pltpu.CompilerParams(dimension_semantics=("parallel", "parallel", "arbitrary"),
                                             vmem_limit_bytes=VMEM_LIMIT),
        name="moba_attn", interpret=interpret,
    )(q_b, k_b, v_b, kmean, blk_onehot_t)


def _slab_rows(ref, row0, n):
    return jnp.concatenate([ref[pl.ds(row0 * ROW_SLABS + s, n, stride=ROW_SLABS), :] for s in range(ROW_SLABS)],
                           axis=1)


def _store_slab_rows(ref, val):
    for s in range(ROW_SLABS):
        ref[pl.ds(s, val.shape[0], stride=ROW_SLABS), :] = val[:, s * LANES:(s + 1) * LANES]


ROUTE_GROUP0 = 0
ROUTE_EXPERT0 = N_GROUPS
ROUTE_EID0, ROUTE_W0, ROUTE_RANK0 = 0, 2, 4


def _first_lane_of_max(vals, lane):
    mx = jnp.max(vals, axis=-1, keepdims=True)
    return mx, jnp.min(jnp.where(vals == mx, lane, LANES), axis=-1, keepdims=True)


def _mix_body(oa_ref, ob_ref, ga_ref, gb_ref, x_ref, woa_ref, wob_ref, wout_ref, fn_ref, wr_ref, br_ref,
              x1_ref, h2_ref, route_ref, counts_ref, carry_ref):
    tm = x_ref.shape[0]

    @pl.when(pl.program_id(0) == 0)
    def _():
        carry_ref[...] = jnp.zeros_like(carry_ref)

    a = _dot(oa_ref[...], woa_ref[...])
    b = _dot(ob_ref[...], wob_ref[...])
    mixed = jax.nn.sigmoid(ga_ref[...].astype(F32)) * a + jax.nn.sigmoid(gb_ref[...].astype(F32)) * b
    x1 = x_ref[...] + _dot(mixed.astype(BF16), wout_ref[...])
    x1_ref[...] = x1
    h2 = _rms(x1, fn_ref[...])
    _store_slab_rows(h2_ref, h2)

    h_hi = h2.astype(BF16)
    h_lo = (h2 - h_hi.astype(F32)).astype(BF16)
    both = _dot(h_hi, wr_ref[...])
    logits = both[:, :LANES] + both[:, LANES:] + _dot(h_lo, wr_ref[:, :LANES]) + br_ref[...]
    lane = lax.broadcasted_iota(jnp.int32, (tm, LANES), 1)
    neg = -jnp.inf
    g_log = jnp.where(lane < N_GROUPS, logits, neg)
    g_max, g_sel = _first_lane_of_max(g_log, lane)
    p_group = 1.0 / jnp.sum(jnp.exp(g_log - g_max), axis=-1, keepdims=True)
    e_lo = ROUTE_EXPERT0 + g_sel * EXPERTS_PER_GROUP
    e_log = jnp.where((lane >= e_lo) & (lane < e_lo + EXPERTS_PER_GROUP), logits, neg)
    v1, i1 = _first_lane_of_max(e_log, lane)
    v2, i2 = _first_lane_of_max(jnp.where(lane == i1, neg, e_log), lane)
    e2 = jnp.exp(v2 - v1)
    w1 = p_group * (1.0 / (1.0 + e2))
    w2 = p_group * (e2 / (1.0 + e2))

    pick1, pick2 = lane == i1, lane == i2
    onehot = (pick1 | pick2).astype(F32)
    tri = (lax.broadcasted_iota(jnp.int32, (tm, tm), 1) < lax.broadcasted_iota(jnp.int32, (tm, tm), 0))
    before = _dot(tri.astype(BF16), onehot.astype(BF16)) + carry_ref[0:1, :]
    r1 = jnp.sum(jnp.where(pick1, before, 0.0), axis=-1, keepdims=True)
    r2 = jnp.sum(jnp.where(pick2, before, 0.0), axis=-1, keepdims=True)
    new_carry = carry_ref[0:1, :] + jnp.sum(onehot, axis=0, keepdims=True)
    carry_ref[...] = jnp.broadcast_to(new_carry, carry_ref.shape)
    counts_ref[...] = jnp.broadcast_to(new_carry, counts_ref.shape)

    fi1 = (i1 - ROUTE_EXPERT0).astype(F32)
    fi2 = (i2 - ROUTE_EXPERT0).astype(F32)
    route = jnp.zeros((tm, LANES), F32)
    for k, val in ((ROUTE_EID0, fi1), (ROUTE_EID0 + 1, fi2), (ROUTE_W0, w1), (ROUTE_W0 + 1, w2),
                   (ROUTE_RANK0, r1), (ROUTE_RANK0 + 1, r2)):
        route = jnp.where(lane == k, val, route)
    route_ref[...] = route


def _mix_call(o_a, o_b, g_a, g_b, x2d, w_oa, w_ob, w_out, ffn_norm, w_router, b_router, tm, interpret):
    T = x2d.shape[0]
    row = lambda i: (i, 0)
    const = lambda i: (0, 0)
    full = lambda a: pl.BlockSpec(a.shape, const, pipeline_mode=pl.Buffered(1))
    return pl.pallas_call(
        _mix_body, grid=(T // tm,),
        in_specs=[pl.BlockSpec((tm, 512), row), pl.BlockSpec((tm, 512), row),
                  pl.BlockSpec((tm, D_MODEL), row), pl.BlockSpec((tm, D_MODEL), row),
                  pl.BlockSpec((tm, D_MODEL), row), full(w_oa), full(w_ob), full(w_out), full(ffn_norm),
                  full(w_router), full(b_router)],
        out_specs=[pl.BlockSpec((tm, D_MODEL), row), pl.BlockSpec((tm * ROW_SLABS, LANES), row),
                   pl.BlockSpec((tm, LANES), row), pl.BlockSpec((8, LANES), const)],
        out_shape=[jax.ShapeDtypeStruct((T, D_MODEL), F32), jax.ShapeDtypeStruct((T * ROW_SLABS, LANES), F32),
                   jax.ShapeDtypeStruct((T, LANES), F32), jax.ShapeDtypeStruct((8, LANES), F32)],
        scratch_shapes=[pltpu.VMEM((8, LANES), F32)],
        compiler_params=pltpu.CompilerParams(dimension_semantics=("arbitrary",), vmem_limit_bytes=VMEM_LIMIT),
        name="mix_route", interpret=interpret,
    )(o_a, o_b, g_a, g_b, x2d, w_oa, w_ob, w_out, ffn_norm, w_router, b_router)


class _RowRing:
    def __init__(self, step, n_steps, src_hbm, idx_ref, idx_next_ref, buf_ref, sems):
        self.step, self.n_steps, self.src = step, n_steps, src_hbm
        self.idx, self.idx_next, self.buf, self.sems = idx_ref, idx_next_ref, buf_ref, sems
        self.slot = step % 2
        self.n_rows = buf_ref.shape[1] // ROW_SLABS

    def _start(self, idx_ref, slot, lo, hi):
        for r in range(lo, hi):
            src = self.src.at[pl.ds(pl.multiple_of(idx_ref[0, 0, r] * ROW_SLABS, ROW_SLABS), ROW_SLABS), :]
            dst = self.buf.at[slot, pl.ds(r * ROW_SLABS, ROW_SLABS), :]
            pltpu.make_async_copy(src, dst, self.sems.at[slot]).start(priority=r % 2)

    def _wait(self, slot):
        pltpu.make_async_copy(self.buf.at[slot], self.buf.at[slot], self.sems.at[slot]).wait()

    def rows(self):
        @pl.when(self.step == 0)
        def _():
            self._start(self.idx, 0, 0, self.n_rows)

        self._wait(self.slot)
        return self.buf.at[self.slot]

    def start_next(self, part, n_parts):
        per = self.n_rows // n_parts
        self._start(self.idx_next, 1 - self.slot, part * per, (part + 1) * per)

    def drain(self):
        @pl.when(self.step == self.n_steps - 1)
        def _():
            self._wait(1 - self.slot)


def _moe_body(blk_e_ref, n_used_ref, tok_ref, tok_next_ref, h_hbm, wg_ref, wu_ref, wd_ref, y_ref, xs_ref, sems):
    del blk_e_ref
    step = pl.program_id(0)
    n_used = n_used_ref[0]

    @pl.when(step < n_used)
    def _():
        ring = _RowRing(step, n_used, h_hbm, tok_ref, tok_next_ref, xs_ref, sems)
        xs = _slab_rows(ring.rows(), 0, ring.n_rows).astype(BF16)
        ring.start_next(0, 4)
        gate = _dot(xs, wg_ref[0])
        ring.start_next(1, 4)
        up = _dot(xs, wu_ref[0])
        ring.start_next(2, 4)
        ys = _dot((jax.nn.silu(gate) * up).astype(BF16), wd_ref[0])
        ring.start_next(3, 4)
        _store_slab_rows(y_ref, ys)
        ring.drain()

    @pl.when(step >= n_used)
    def _():
        y_ref[...] = jnp.zeros_like(y_ref)


def _moe_call(blk_e, n_used, slot_tok, h2, w_g, w_u, w_d, interpret):
    n_blocks = slot_tok.shape[0]
    nb = EXPERT_BLOCK
    tok_spec = lambda off: pl.BlockSpec((1, 1, nb), lambda i, be, nu: (jnp.minimum(i + off, n_blocks - 1), 0, 0),
                                        memory_space=pltpu.SMEM)
    grid_spec = pltpu.PrefetchScalarGridSpec(
        num_scalar_prefetch=2, grid=(n_blocks,),
        in_specs=[tok_spec(0), tok_spec(1),
                  pl.BlockSpec(memory_space=pl.ANY),
                  pl.BlockSpec((1, D_MODEL, EXPERT_FF), lambda i, be, nu: (be[i], 0, 0)),
                  pl.BlockSpec((1, D_MODEL, EXPERT_FF), lambda i, be, nu: (be[i], 0, 0)),
                  pl.BlockSpec((1, EXPERT_FF, D_MODEL), lambda i, be, nu: (be[i], 0, 0))],
        out_specs=pl.BlockSpec((nb * ROW_SLABS, LANES), lambda i, be, nu: (i, 0)),
        scratch_shapes=[pltpu.VMEM((2, nb * ROW_SLABS, LANES), F32), pltpu.SemaphoreType.DMA((2,))],
    )
    return pl.pallas_call(
        _moe_body, grid_spec=grid_spec,
        out_shape=jax.ShapeDtypeStruct((n_blocks * nb * ROW_SLABS, LANES), F32),
        compiler_params=pltpu.CompilerParams(dimension_semantics=("arbitrary",), vmem_limit_bytes=VMEM_LIMIT),
        name="moe_experts", interpret=interpret,
    )(blk_e, n_used, slot_tok, slot_tok, h2, w_g, w_u, w_d)


def _final_body(dest_ref, dest_next_ref, x1_ref, route_ref, y_hbm, g_ref, o_ref, ybuf_ref, sems):
    tm = x1_ref.shape[0]
    ring = _RowRing(pl.program_id(0), pl.num_programs(0), y_hbm, dest_ref, dest_next_ref, ybuf_ref, sems)
    rows = ring.rows()
    rc = 64
    n_parts = tm // rc
    for c in range(n_parts):
        sl = pl.ds(c * rc, rc)
        route = route_ref[sl, :]
        lane = lax.broadcasted_iota(jnp.int32, route.shape, 1)
        w1 = jnp.sum(jnp.where(lane == ROUTE_W0, route, 0.0), axis=-1, keepdims=True)
        w2 = jnp.sum(jnp.where(lane == ROUTE_W0 + 1, route, 0.0), axis=-1, keepdims=True)
        y = x1_ref[sl, :] + w1 * _slab_rows(rows, c * rc, rc) + w2 * _slab_rows(rows, tm + c * rc, rc)
        o_ref[sl, :] = _rms(y, g_ref[...])
        ring.start_next(c, n_parts)
    ring.drain()


def _final_call(dest, x1, route, y, final_norm, tm, interpret):
    T = x1.shape[0]
    nt = T // tm
    dest_spec = lambda off: pl.BlockSpec((1, 1, EXPERT_TOPK * tm), lambda i: (jnp.minimum(i + off, nt - 1), 0, 0),
                                         memory_space=pltpu.SMEM)
    return pl.pallas_call(
        _final_body, grid=(nt,),
        in_specs=[dest_spec(0), dest_spec(1),
                  pl.BlockSpec((tm, D_MODEL), lambda i: (i, 0)),
                  pl.BlockSpec((tm, LANES), lambda i: (i, 0)),
                  pl.BlockSpec(memory_space=pl.ANY),
                  pl.BlockSpec((1, D_MODEL), lambda i: (0, 0))],
        out_specs=pl.BlockSpec((tm, D_MODEL), lambda i: (i, 0)),
        out_shape=jax.ShapeDtypeStruct((T, D_MODEL), F32),
        scratch_shapes=[pltpu.VMEM((2, EXPERT_TOPK * tm * ROW_SLABS, LANES), F32), pltpu.SemaphoreType.DMA((2,))],
        compiler_params=pltpu.CompilerParams(dimension_semantics=("arbitrary",), vmem_limit_bytes=VMEM_LIMIT),
        name="final_norm", interpret=interpret,
    )(dest, dest, x1, route, y, final_norm)


def _rope_tables(seq, rot_dim):
    half = rot_dim // 2
    inv_freq = jnp.power(ROPE_THETA, -jnp.arange(half, dtype=F32) / half)
    ang = jnp.arange(seq, dtype=F32)[:, None] * inv_freq[None, :]
    return jnp.cos(ang), jnp.sin(ang)


def _lane_tables(seq):
    cos, sin = _rope_tables(seq, MLA_ROPE)
    z = lambda n: jnp.zeros((seq, n), F32)
    o = lambda n: jnp.ones((seq, n), F32)
    mla = (jnp.concatenate([o(64), cos, cos, o(32)], 1),
           jnp.concatenate([z(64), -sin, z(48)], 1),
           jnp.concatenate([z(80), sin, z(32)], 1))
    cos, sin = _rope_tables(seq, MOBA_ROT)
    head = (jnp.concatenate([cos, cos, o(48)], 1), jnp.concatenate([-sin, z(56)], 1),
            jnp.concatenate([z(8), sin, z(48)], 1))
    moba = tuple(jnp.concatenate([t, t], 1) for t in head)
    return mla + moba


def _prep_weights(w_in, w_uq, w_ukv):
    kr = jnp.pad(w_in[:, 1024:1056], ((0, 0), (64, 32)))
    w_in_r = jnp.concatenate([w_in[:, :1024], kr, w_in[:, 1056:]], axis=1).astype(BF16)
    w_uq_r = jnp.pad(w_uq.reshape(Q_LORA, MLA_HEADS, MLA_NOPE + MLA_ROPE),
                     ((0, 0), (0, 0), (0, LANES - MLA_NOPE - MLA_ROPE))).reshape(Q_LORA, MLA_HEADS * LANES)
    w_kv = w_ukv.reshape(KV_LORA, MLA_HEADS, MLA_NOPE + MLA_V)
    w_uk_r = jnp.pad(w_kv[:, :, :MLA_NOPE], ((0, 0), (0, 0), (0, LANES - MLA_NOPE))).reshape(KV_LORA, -1)
    w_uv_r = w_kv[:, :, MLA_NOPE:].reshape(KV_LORA, MLA_HEADS * MLA_V)
    return w_in_r, w_uq_r.astype(BF16), w_uk_r.astype(BF16), w_uv_r.astype(BF16)


def _router_weights(w_rg, b_rg, w_re, b_re):
    w_e = jnp.transpose(w_re, (1, 0, 2)).reshape(D_MODEL, N_EXPERTS)
    w = jnp.concatenate([w_rg, w_e], axis=1)
    b = jnp.concatenate([b_rg, b_re.reshape(N_EXPERTS)])
    pad = LANES - N_GROUPS - N_EXPERTS
    w = jnp.pad(w, ((0, 0), (0, pad)))
    w_hi = w.astype(BF16)
    w_lo = (w - w_hi.astype(F32)).astype(BF16)
    return jnp.concatenate([w_hi, w_lo], axis=1), jnp.pad(b, (0, pad))[None, :]


def _dispatch(route, counts, T, tm):
    eid = route[:, ROUTE_EID0:ROUTE_EID0 + EXPERT_TOPK].astype(jnp.int32)
    rank = route[:, ROUTE_RANK0:ROUTE_RANK0 + EXPERT_TOPK].astype(jnp.int32)
    cnt = counts[0, ROUTE_EXPERT0:ROUTE_EXPERT0 + N_EXPERTS].astype(jnp.int32)
    padded = (cnt + EXPERT_BLOCK - 1) // EXPERT_BLOCK * EXPERT_BLOCK
    pend = jnp.cumsum(padded)
    pstart = pend - padded
    dest = pstart[eid] + rank
    n_blocks = T * EXPERT_TOPK // EXPERT_BLOCK + N_EXPERTS
    n_slots = n_blocks * EXPERT_BLOCK
    tok = jnp.repeat(jnp.arange(T, dtype=jnp.int32), EXPERT_TOPK)
    slot_tok = jnp.zeros((n_slots,), jnp.int32).at[dest.reshape(-1)].set(
        tok, unique_indices=True, mode='promise_in_bounds')
    blk_start = jnp.arange(n_blocks, dtype=jnp.int32) * EXPERT_BLOCK
    blk_e = jnp.minimum(jnp.sum(pend[None, :] <= blk_start[:, None], axis=1), N_EXPERTS - 1).astype(jnp.int32)
    n_used = (pend[-1] // EXPERT_BLOCK).astype(jnp.int32)[None]
    dest_tiles = dest.reshape(T // tm, tm, EXPERT_TOPK).transpose(0, 2, 1).reshape(T // tm, 1, EXPERT_TOPK * tm)
    return blk_e, n_used, slot_tok.reshape(n_blocks, 1, EXPERT_BLOCK), dest_tiles


def _forward(x, attn_norm, w_in, q_norm, w_uq, kv_norm, w_ukv, w_o_mla, w_o_moba, w_out, ffn_norm,
             w_router_group, b_router_group, w_router_expert, b_router_expert, w_exp_gate, w_exp_up,
             w_exp_down, final_norm, *, tm=512, tm_dense=512, tq_mla=512, tk_mla=1024, moba_group=4,
             pairs_per_step=2, interpret=False):
    B, S, _ = x.shape
    T = B * S
    x2d = x.reshape(T, D_MODEL)
    tabs = _lane_tables(S)
    for l in range(attn_norm.shape[0]):
        w_in_r, w_uq_r, w_uk_r, w_uv_r = _prep_weights(w_in[l], w_uq[l], w_ukv[l])
        q_m, k_m, v_m, q_b, k_b, v_b, kmean, g_a, g_b = _proj_call(
            x2d, attn_norm[l][None], w_in_r, q_norm[l][None], w_uq_r, kv_norm[l][None], w_uk_r, w_uv_r,
            tabs, S, tm_dense, moba_group * MOBA_BLOCK, interpret)
        o_a = _mla_call(q_m, k_m, v_m, B, S, tq_mla, tk_mla, pairs_per_step, interpret)
        o_b = _moba_call(q_b, k_b, v_b, kmean.reshape(B, S // MOBA_BLOCK, 512), B, S, moba_group, pairs_per_step, interpret)
        w_r, b_r = _router_weights(w_router_group[l], b_router_group[l], w_router_expert[l], b_router_expert[l])
        x1, h2, route, counts = _mix_call(o_a, o_b, g_a, g_b, x2d, w_o_mla[l].astype(BF16),
                                          w_o_moba[l].astype(BF16), w_out[l].astype(BF16), ffn_norm[l][None],
                                          w_r, b_r, tm_dense, interpret)
        blk_e, n_used, slot_tok, dest_tiles = _dispatch(route, counts, T, tm)
        y = _moe_call(blk_e, n_used, slot_tok, h2, w_exp_gate[l].astype(BF16),
                      w_exp_up[l].astype(BF16), w_exp_down[l].astype(BF16), interpret)
        assert attn_norm.shape[0] == 1
    return _final_call(dest_tiles, x1, route, y, final_norm[None], tm, interpret).reshape(B, S, D_MODEL)


def kernel(x, attn_norm, w_in, q_norm, w_uq, kv_norm, w_ukv, w_o_mla, w_o_moba, w_out, ffn_norm,
           w_router_group, b_router_group, w_router_expert, b_router_expert, w_exp_gate, w_exp_up,
           w_exp_down, final_norm):
    return _forward(x, attn_norm, w_in, q_norm, w_uq, kv_norm, w_ukv, w_o_mla, w_o_moba, w_out, ffn_norm,
                    w_router_group, b_router_group, w_router_expert, b_router_expert, w_exp_gate, w_exp_up,
                    w_exp_down, final_norm)
```
